```python
import jax, jax.numpy as jnp
from jax import lax
import numpy as np

D_MODEL = 1024
BATCH = 8
SEQ = 4096
DEPTH = 1

PLE_DIM = 256
D_FF = 2816
LRU_WIDTH = 1024
LRU_BLOCKS = 8
LRU_BLOCK_W = LRU_WIDTH // LRU_BLOCKS
CONV_WIDTH = 4
LRU_C = 8.0
MLA_HEADS = 8
QK_NOPE = 128
QK_ROPE = 64
V_HEAD = 128
Q_LORA = 384
KV_LORA = 256
ROPE_THETA = 10000.0
Q_BLOCK = 128
EPS = 1e-6
IN_SPLITS = (LRU_WIDTH, LRU_WIDTH, Q_LORA, KV_LORA, QK_ROPE, D_MODEL, D_MODEL)
IN_COLS = sum(IN_SPLITS)

kernel_name = "hybrid_rglru_mla_macaron_block"


def rmsnorm(x, g):
    xf = x.astype(jnp.float32)
    y = xf * lax.rsqrt(jnp.mean(xf * xf, axis=-1, keepdims=True) + EPS)
    return (y * g.astype(jnp.float32)).astype(x.dtype)


def swiglu(x, w_gate, w_up, w_down):
    return (jax.nn.silu(x @ w_gate) * (x @ w_up)) @ w_down


def centred_depthwise_conv(x, w, b):
    s = x.shape[1]
    left = CONV_WIDTH // 2
    xp = jnp.pad(x, ((0, 0), (left, CONV_WIDTH - 1 - left), (0, 0)))
    y = b
    for k in range(CONV_WIDTH):
        y = y + xp[:, k:k + s, :] * w[k]
    return y


def _linear_recurrence_combine(left, right):
    a_l, b_l = left
    a_r, b_r = right
    return a_l * a_r, a_r * b_l + b_r


def rg_lru(x, w_r, b_r, w_i, b_i, lam, reverse):
    bsz, s, c = x.shape
    xb = x.reshape(bsz, s, LRU_BLOCKS, LRU_BLOCK_W)
    r = jax.nn.sigmoid(jnp.einsum('bsnc,ncd->bsnd', xb, w_r).reshape(bsz, s, c) + b_r)
    gi = jax.nn.sigmoid(jnp.einsum('bsnc,ncd->bsnd', xb, w_i).reshape(bsz, s, c) + b_i)
    log_a = -LRU_C * r.astype(jnp.float32) * jax.nn.softplus(-lam.astype(jnp.float32))
    a = jnp.exp(log_a)
    u = jnp.sqrt(-jnp.expm1(2.0 * log_a)) * (gi * x).astype(jnp.float32)
    _, h = lax.associative_scan(_linear_recurrence_combine, (a, u), reverse=reverse, axis=1)
    return h.astype(x.dtype)


def rope_cos_sin(positions, dim, dtype):
    inv_freq = ROPE_THETA ** (-jnp.arange(0, dim, 2, dtype=jnp.float32) / dim)
    ang = positions.astype(jnp.float32)[..., None] * inv_freq
    return (jnp.cos(ang)[:, :, None, :].astype(dtype), jnp.sin(ang)[:, :, None, :].astype(dtype))


def apply_rope(x, cos, sin):
    x1, x2 = jnp.split(x, 2, axis=-1)
    return jnp.concatenate([x1 * cos - x2 * sin, x2 * cos + x1 * sin], axis=-1)


def mla(c_q, c_kv, k_rope, positions, q_norm, w_uq, kv_norm, w_ukv):
    bsz, s, _ = c_q.shape
    cos, sin = rope_cos_sin(positions, QK_ROPE, c_q.dtype)
    q = (rmsnorm(c_q, q_norm) @ w_uq).reshape(bsz, s, MLA_HEADS, QK_NOPE + QK_ROPE)
    q = jnp.concatenate([q[..., :QK_NOPE], apply_rope(q[..., QK_NOPE:], cos, sin)], axis=-1)
    kv = (rmsnorm(c_kv, kv_norm) @ w_ukv).reshape(bsz, s, MLA_HEADS, QK_NOPE + V_HEAD)
    k_pe = jnp.broadcast_to(apply_rope(k_rope[:, :, None, :], cos, sin), (bsz, s, MLA_HEADS, QK_ROPE))
    k = jnp.concatenate([kv[..., :QK_NOPE], k_pe], axis=-1)
    v = kv[..., QK_NOPE:]
    scale = (QK_NOPE + QK_ROPE) ** -0.5
    n_blk = s // Q_BLOCK
    q_blocks = q.reshape(bsz, n_blk, Q_BLOCK, MLA_HEADS, QK_NOPE + QK_ROPE).transpose(1, 0, 2, 3, 4)

    def attend(qb):
        sc = jnp.einsum('bqhd,bkhd->bhqk', qb, k).astype(jnp.float32) * scale
        pr = jax.nn.softmax(sc, axis=-1).astype(v.dtype)
        return jnp.einsum('bhqk,bkhd->bqhd', pr, v)

    o = lax.map(attend, q_blocks)
    return o.transpose(1, 0, 2, 3, 4).reshape(bsz, s, MLA_HEADS * V_HEAD)


def _dense(k, shape, fan_in):
    return jax.random.normal(k, shape, jnp.float32) * (fan_in ** -0.5)


def _gain(k, shape):
    return 1.0 + 0.05 * jax.random.normal(k, shape, jnp.float32)


def _bias(k, shape):
    return 0.01 * jax.random.normal(k, shape, jnp.float32)


def setup_inputs(seed: int = 0) -> dict:
    key = jax.random.key(seed)
    ks = jax.random.split(key, 32)
    L = DEPTH
    a_init = jax.random.uniform(ks[15], (L, 2, LRU_WIDTH), jnp.float32, 0.9, 0.999) ** (1.0 / LRU_C)
    lru_lambda = jnp.log(a_init) - jnp.log1p(-a_init)
    return {
        "x": jax.random.normal(ks[0], (BATCH, SEQ, D_MODEL), jnp.float32),
        "p": jax.random.normal(ks[1], (L, BATCH, SEQ, PLE_DIM), jnp.float32),
        "positions": jnp.broadcast_to(jnp.arange(SEQ, dtype=jnp.int32), (BATCH, SEQ)),
        "ffn1_norm": _gain(ks[2], (L, D_MODEL)),
        "ffn1_w_gate": _dense(ks[3], (L, D_MODEL, D_FF), D_MODEL),
        "ffn1_w_up": _dense(ks[4], (L, D_MODEL, D_FF), D_MODEL),
        "ffn1_w_down": _dense(ks[5], (L, D_FF, D_MODEL), D_FF),
        "mix_norm": _gain(ks[6], (L, D_MODEL)),
        "w_in": _dense(ks[7], (L, D_MODEL, IN_COLS), D_MODEL),
        "conv_w": _dense(ks[8], (L, CONV_WIDTH, LRU_WIDTH), CONV_WIDTH),
        "conv_b": _bias(ks[9], (L, LRU_WIDTH)),
        "lru_w_r": _dense(ks[10], (L, 2, LRU_BLOCKS, LRU_BLOCK_W, LRU_BLOCK_W), LRU_BLOCK_W),
        "lru_b_r": _bias(ks[11], (L, 2, LRU_WIDTH)),
        "lru_w_i": _dense(ks[12], (L, 2, LRU_BLOCKS, LRU_BLOCK_W, LRU_BLOCK_W), LRU_BLOCK_W),
        "lru_b_i": _bias(ks[13], (L, 2, LRU_WIDTH)),
        "lru_lambda": lru_lambda,
        "w_lru_out": _dense(ks[14], (L, LRU_WIDTH, D_MODEL), LRU_WIDTH),
        "q_norm": _gain(ks[16], (L, Q_LORA)),
        "w_uq": _dense(ks[17], (L, Q_LORA, MLA_HEADS * (QK_NOPE + QK_ROPE)), Q_LORA),
        "kv_norm": _gain(ks[18], (L, KV_LORA)),
        "w_ukv": _dense(ks[19], (L, KV_LORA, MLA_HEADS * (QK_NOPE + V_HEAD)), KV_LORA),
        "w_mla_out": _dense(ks[20], (L, MLA_HEADS * V_HEAD, D_MODEL), MLA_HEADS * V_HEAD),
        "w_o": _dense(ks[21], (L, D_MODEL, D_MODEL), D_MODEL),
        "ffn2_norm": _gain(ks[22], (L, D_MODEL)),
        "ffn2_w_gate": _dense(ks[23], (L, D_MODEL, D_FF), D_MODEL),
        "ffn2_w_up": _dense(ks[24], (L, D_MODEL, D_FF), D_MODEL),
        "ffn2_w_down": _dense(ks[25], (L, D_FF, D_MODEL), D_FF),
        "ple_norm": _gain(ks[26], (L, D_MODEL)),
        "ple_w_gate": _dense(ks[27], (L, D_MODEL, D_MODEL), D_MODEL),
        "ple_w_proj": _dense(ks[28], (L, PLE_DIM, D_MODEL), PLE_DIM),
        "ple_proj_norm": _gain(ks[29], (L, D_MODEL)),
        "final_norm": _gain(ks[30], (D_MODEL,)),
    }


def reference(x, p, positions, ffn1_norm, ffn1_w_gate, ffn1_w_up, ffn1_w_down, mix_norm, w_in,
              conv_w, conv_b, lru_w_r, lru_b_r, lru_w_i, lru_b_i, lru_lambda, w_lru_out,
              q_norm, w_uq, kv_norm, w_ukv, w_mla_out, w_o, ffn2_norm, ffn2_w_gate, ffn2_w_up,
              ffn2_w_down, ple_norm, ple_w_gate, ple_w_proj, ple_proj_norm, final_norm):
    split_at = np.cumsum(IN_SPLITS)[:-1].tolist()
    h = x
    for i in range(DEPTH):
        h = h + 0.5 * swiglu(rmsnorm(h, ffn1_norm[i]), ffn1_w_gate[i], ffn1_w_up[i], ffn1_w_down[i])
        u = rmsnorm(h, mix_norm[i])
        z_x, z_g, c_q, c_kv, k_rope, gate_a, gate_b = jnp.split(u @ w_in[i], split_at, axis=-1)
        xa = centred_depthwise_conv(z_x, conv_w[i], conv_b[i])
        ya = (rg_lru(xa, lru_w_r[i, 0], lru_b_r[i, 0], lru_w_i[i, 0], lru_b_i[i, 0], lru_lambda[i, 0], False)
              + rg_lru(xa, lru_w_r[i, 1], lru_b_r[i, 1], lru_w_i[i, 1], lru_b_i[i, 1], lru_lambda[i, 1], True))
        ya = (ya * jax.nn.gelu(z_g)) @ w_lru_out[i]
        yb = mla(c_q, c_kv, k_rope, positions, q_norm[i], w_uq[i], kv_norm[i], w_ukv[i]) @ w_mla_out[i]
        merged = jax.nn.sigmoid(gate_a) * ya + jax.nn.sigmoid(gate_b) * yb
        h = h + merged @ w_o[i]
        h = h + 0.5 * swiglu(rmsnorm(h, ffn2_norm[i]), ffn2_w_gate[i], ffn2_w_up[i], ffn2_w_down[i])
        ple_gate = jax.nn.sigmoid(rmsnorm(h, ple_norm[i]) @ ple_w_gate[i])
        h = h + ple_gate * rmsnorm(p[i] @ ple_w_proj[i], ple_proj_norm[i])
    return rmsnorm(h, final_norm)
```

```python
import functools
import math

import jax
import jax.numpy as jnp
from jax import lax
from jax.experimental import pallas as pl
from jax.experimental.pallas import tpu as pltpu

F32 = jnp.float32
BF16 = jnp.bfloat16

LRU_BLOCKS = 8
CONV_WIDTH = 4
LRU_C = 8.0
MLA_HEADS = 8
QK_NOPE = 128
QK_ROPE = 64
V_HEAD = 128
ROPE_THETA = 10000.0
EPS = 1e-6

LANES = 128
SUBLANES = 8
QK_PAD = 256
VMEM_LIMIT_BYTES = 56 * 1024 * 1024

ROW_TILE = 256
LRU_STEPS = 64
Q_TILE = 256
FF_CHUNK = 512


def _resident(shape):
    nd = len(shape)
    return pl.BlockSpec(shape, lambda *_: (0,) * nd, pipeline_mode=pl.Buffered(1))


def _params(n_axes, parallel=True):
    sem = ("parallel" if parallel else "arbitrary",) * n_axes
    return pltpu.CompilerParams(dimension_semantics=sem, vmem_limit_bytes=VMEM_LIMIT_BYTES)


def _rmsnorm(x, g):
    ms = jnp.mean(x * x, axis=-1, keepdims=True)
    return x * lax.rsqrt(ms + EPS) * g


def _dot(a, b):
    return jnp.dot(a, b, preferred_element_type=F32)


def _swiglu_half(x, g, wg_ref, wu_ref, wd_ref):
    d_ff = wg_ref.shape[1]
    xn = _rmsnorm(x, g).astype(BF16)
    y = None
    for c0 in range(0, d_ff, FF_CHUNK):
        c1 = min(c0 + FF_CHUNK, d_ff)
        gate = _dot(xn, wg_ref[:, c0:c1])
        up = _dot(xn, wu_ref[:, c0:c1])
        act = (gate * jax.nn.sigmoid(gate) * up).astype(BF16)
        part = _dot(act, wd_ref[c0:c1, :])
        y = part if y is None else y + part
    return x + 0.5 * y


def _ffn_kernel(x_ref, g_ref, wg_ref, wu_ref, wd_ref, o_ref):
    o_ref[0] = _swiglu_half(x_ref[0], g_ref[...], wg_ref, wu_ref, wd_ref)


def _ffn(x, g, wg, wu, wd):
    b, s, d = x.shape
    t = ROW_TILE
    row = pl.BlockSpec((1, t, d), lambda i, j: (i, j, 0))
    return pl.pallas_call(
        _ffn_kernel,
        out_shape=jax.ShapeDtypeStruct((b, s, d), F32),
        grid=(b, s // t),
        in_specs=[row, _resident(g.shape), _resident(wg.shape), _resident(wu.shape),
                  _resident(wd.shape)],
        out_specs=row,
        compiler_params=_params(2),
        name="ffn",
    )(x, g, wg, wu, wd)


def _inproj_kernel(h_ref, pos_ref, g_ref, w_ref, qg_ref, wqm_ref, wqs_ref, kvg_ref, wk_ref,
                   wv_ref, invf_ref, sign_ref,
                   zx_ref, zg_ref, ga_ref, gb_ref, q_ref, k_ref, v_ref, *, cols, q_scale):
    un = _rmsnorm(h_ref[0], g_ref[...]).astype(BF16)

    def proj(name):
        c0, c1 = cols[name]
        return _dot(un, w_ref[:, c0:c1])

    zx_ref[...] = proj("zx")
    zg_ref[...] = proj("zg").astype(BF16)
    ga_ref[0] = proj("ga").astype(BF16)
    gb_ref[0] = proj("gb").astype(BF16)

    ang = pos_ref[0] * invf_ref[...]
    cosv = jnp.cos(ang)
    sinv = jnp.sin(ang) * sign_ref[...]

    cqn = _rmsnorm(proj("cq"), qg_ref[...]).astype(BF16)
    qm = _dot(cqn, wqm_ref[...])
    qs = _dot(cqn, wqs_ref[...])
    cq_cos = cosv * q_scale
    cq_sin = sinv * q_scale
    for h in range(MLA_HEADS):
        m0 = h * QK_PAD
        q_ref[0, h, :, 0:QK_NOPE] = (qm[:, m0:m0 + QK_NOPE] * q_scale).astype(BF16)
        rope = qm[:, m0 + QK_NOPE:m0 + QK_PAD] * cq_cos + qs[:, h * LANES:(h + 1) * LANES] * cq_sin
        q_ref[0, h, :, QK_NOPE:QK_PAD] = rope.astype(BF16)

    ckvn = _rmsnorm(proj("ckv"), kvg_ref[...]).astype(BF16)
    kn = _dot(ckvn, wk_ref[...])
    vv = _dot(ckvn, wv_ref[...])
    kr = proj("kr")
    kpe = (kr[:, 0:LANES] * cosv + kr[:, LANES:2 * LANES] * sinv).astype(BF16)
    for h in range(MLA_HEADS):
        k_ref[0, h, :, 0:QK_NOPE] = kn[:, h * QK_NOPE:(h + 1) * QK_NOPE].astype(BF16)
        k_ref[0, h, :, QK_NOPE:QK_PAD] = kpe
        v_ref[0, h] = vv[:, h * V_HEAD:(h + 1) * V_HEAD].astype(BF16)


def _inproj(h1, posf, g, w, cols, qg, wqm, wqs, kvg, wk, wv, invf, sign, q_scale):
    b, s, d = h1.shape
    t = ROW_TILE
    width = cols["zx"][1] - cols["zx"][0]
    bm = lambda x: pl.BlockSpec((1, t, x), lambda i, j: (i, j, 0))
    tm = lambda x: pl.BlockSpec((t, x), lambda i, j: (j, i))
    heads = lambda x: pl.BlockSpec((1, MLA_HEADS, t, x), lambda i, j: (i, 0, j, 0))
    kernel = functools.partial(_inproj_kernel, cols=cols, q_scale=q_scale)
    return pl.pallas_call(
        kernel,
        out_shape=[
            jax.ShapeDtypeStruct((s, b * width), F32),
            jax.ShapeDtypeStruct((s, b * width), BF16),
            jax.ShapeDtypeStruct((b, s, d), BF16),
            jax.ShapeDtypeStruct((b, s, d), BF16),
            jax.ShapeDtypeStruct((b, MLA_HEADS, s, QK_PAD), BF16),
            jax.ShapeDtypeStruct((b, MLA_HEADS, s, QK_PAD), BF16),
            jax.ShapeDtypeStruct((b, MLA_HEADS, s, V_HEAD), BF16),
        ],
        grid=(b, s // t),
        in_specs=[bm(d), bm(1)] + [_resident(a.shape) for a in
                                   (g, w, qg, wqm, wqs, kvg, wk, wv, invf, sign)],
        out_specs=[tm(width), tm(width), bm(d), bm(d), heads(QK_PAD), heads(QK_PAD),
                   heads(V_HEAD)],
        compiler_params=_params(2),
        name="inproj",
    )(h1, posf, g, w, qg, wqm, wqs, kvg, wk, wv, invf, sign)


def _lru_kernel(*refs, reverse, n_tiles, steps):
    if reverse:
        (prev_ref, cur_ref, next_ref, cw_ref, cb_ref, wg_ref, br_ref, bi_ref, lam_ref,
         hf_ref, zg_ref, o_ref, xpad, a_s, u_s, h_s, hb_s) = refs
    else:
        (prev_ref, cur_ref, next_ref, cw_ref, cb_ref, wg_ref, br_ref, bi_ref, lam_ref,
         o_ref, xpad, a_s, u_s, h_s) = refs
        hb_s = o_ref
    i = pl.program_id(0)
    tile = (n_tiles - 1 - i) if reverse else i
    rows = steps * SUBLANES
    halo = (CONV_WIDTH // 2) * SUBLANES
    width = cur_ref.shape[1]
    blk = width // LRU_BLOCKS

    @pl.when(i == 0)
    def _():
        h_s[...] = jnp.zeros_like(h_s)

    xpad[0:halo, :] = jnp.where(tile > 0, prev_ref[...], 0.0)
    xpad[halo:halo + rows, :] = cur_ref[...]
    xpad[halo + rows:halo + rows + SUBLANES, :] = jnp.where(
        tile < n_tiles - 1, next_ref[0:SUBLANES, :], 0.0)

    xa = cb_ref[...] + xpad[0:rows, :] * cw_ref[0:1, :]
    for k in range(1, CONV_WIDTH):
        xa = xa + xpad[k * SUBLANES:k * SUBLANES + rows, :] * cw_ref[k:k + 1, :]

    nlam = -lam_ref[...]
    softplus = jnp.maximum(nlam, 0.0) + jnp.log1p(jnp.exp(-jnp.abs(nlam)))
    xa_b = xa.astype(BF16)
    for n in range(LRU_BLOCKS):
        c0, c1 = n * blk, (n + 1) * blk
        gates = _dot(xa_b[:, c0:c1], wg_ref[n])
        r = jax.nn.sigmoid(gates[:, 0:blk] + br_ref[:, c0:c1])
        gi = jax.nn.sigmoid(gates[:, blk:2 * blk] + bi_ref[:, c0:c1])
        log_a = (-LRU_C) * r * softplus[:, c0:c1]
        a_s[:, c0:c1] = jnp.exp(log_a)
        u_s[:, c0:c1] = jnp.sqrt(1.0 - jnp.exp(2.0 * log_a)) * (gi * xa[:, c0:c1])

    def step(j, h):
        t = (steps - 1 - j) if reverse else j
        r0 = pl.multiple_of(t * SUBLANES, SUBLANES)
        h = a_s[pl.ds(r0, SUBLANES), :] * h + u_s[pl.ds(r0, SUBLANES), :]
        hb_s[pl.ds(r0, SUBLANES), :] = h
        return h

    h_s[...] = lax.fori_loop(0, steps, step, h_s[...], unroll=8)

    if reverse:
        zg = zg_ref[...].astype(F32)
        o_ref[...] = ((hf_ref[...] + hb_s[...]) * jax.nn.gelu(zg)).astype(o_ref.dtype)


def _lru(zx2, cw, cb, wg, br, bi, lam, reverse, hf2=None, zg2=None):
    n_rows, width = zx2.shape
    steps = LRU_STEPS
    rows = steps * SUBLANES
    n_tiles = n_rows // rows
    halo = (CONV_WIDTH // 2) * SUBLANES
    per_halo = rows // halo
    last_halo = n_rows // halo - 1

    def tile_of(i):
        return (n_tiles - 1 - i) if reverse else i

    cur = pl.BlockSpec((rows, width), lambda i: (tile_of(i), 0))
    prev = pl.BlockSpec((halo, width), lambda i: (jnp.maximum(tile_of(i) * per_halo - 1, 0), 0))
    nxt = pl.BlockSpec((halo, width),
                       lambda i: (jnp.minimum((tile_of(i) + 1) * per_halo, last_halo), 0))
    weights = (cw, cb, wg, br, bi, lam)
    in_specs = [prev, cur, nxt] + [_resident(a.shape) for a in weights]
    args = [zx2, zx2, zx2, *weights]
    scratch = [
        pltpu.VMEM((rows + halo + SUBLANES, width), F32),
        pltpu.VMEM((rows, width), F32),
        pltpu.VMEM((rows, width), F32),
        pltpu.VMEM((SUBLANES, width), F32),
    ]
    if reverse:
        in_specs += [cur, cur]
        args += [hf2, zg2]
        scratch.append(pltpu.VMEM((rows, width), F32))
        out_dtype = BF16
    else:
        out_dtype = F32
    kernel = functools.partial(_lru_kernel, reverse=reverse, n_tiles=n_tiles, steps=steps)
    return pl.pallas_call(
        kernel,
        out_shape=jax.ShapeDtypeStruct((n_rows, width), out_dtype),
        grid=(n_tiles,),
        in_specs=in_specs,
        out_specs=cur,
        scratch_shapes=scratch,
        compiler_params=_params(1, parallel=False),
        name="lru_bwd" if reverse else "lru_fwd",
    )(*args)


def _attn_kernel(q_ref, k_ref, v_ref, o_ref):
    s = lax.dot_general(q_ref[0, 0], k_ref[0, 0], (((1,), (1,)), ((), ())),
                        preferred_element_type=F32)
    m = jnp.max(s, axis=-1, keepdims=True)
    p = jnp.exp2(s - m)
    l = jnp.sum(p, axis=-1, keepdims=True)
    o = _dot(p.astype(BF16), v_ref[0, 0])
    o_ref[0] = (o / l).astype(o_ref.dtype)


def _attn(q, k, v):
    b, nh, s, dq = q.shape
    dv = v.shape[-1]
    tq = Q_TILE
    return pl.pallas_call(
        _attn_kernel,
        out_shape=jax.ShapeDtypeStruct((b, s, nh * dv), BF16),
        grid=(b, nh, s // tq),
        in_specs=[
            pl.BlockSpec((1, 1, tq, dq), lambda i, h, j: (i, h, j, 0)),
            pl.BlockSpec((1, 1, s, dq), lambda i, h, j: (i, h, 0, 0)),
            pl.BlockSpec((1, 1, s, dv), lambda i, h, j: (i, h, 0, 0)),
        ],
        out_specs=pl.BlockSpec((1, tq, dv), lambda i, h, j: (i, j, h)),
        compiler_params=_params(3),
        name="attn",
    )(q, k, v)


def _merge_kernel(h_ref, ya_ref, ob_ref, ga_ref, gb_ref, wl_ref, wm_ref, wo_ref, o_ref):
    ya = _dot(ya_ref[...], wl_ref[...])
    yb = _dot(ob_ref[0], wm_ref[...])
    merged = (jax.nn.sigmoid(ga_ref[0].astype(F32)) * ya
              + jax.nn.sigmoid(gb_ref[0].astype(F32)) * yb)
    o_ref[0] = h_ref[0] + _dot(merged.astype(BF16), wo_ref[...])


def _merge(h1, ya_tm, ob, ga, gb, wl, wm, wo):
    b, s, d = h1.shape
    t = ROW_TILE
    bm = pl.BlockSpec((1, t, d), lambda i, j: (i, j, 0))
    tm = pl.BlockSpec((t, d), lambda i, j: (j, i))
    return pl.pallas_call(
        _merge_kernel,
        out_shape=jax.ShapeDtypeStruct((b, s, d), F32),
        grid=(b, s // t),
        in_specs=[bm, tm, bm, bm, bm, _resident(wl.shape), _resident(wm.shape),
                  _resident(wo.shape)],
        out_specs=bm,
        compiler_params=_params(2),
        name="merge",
    )(h1, ya_tm, ob, ga, gb, wl, wm, wo)


def _ffn_ple_kernel(x_ref, p_ref, g_ref, wg_ref, wu_ref, wd_ref, pg_ref, wpg_ref, wpp_ref,
                    ppg_ref, fg_ref, o_ref):
    h = _swiglu_half(x_ref[0], g_ref[...], wg_ref, wu_ref, wd_ref)
    gate = jax.nn.sigmoid(_dot(_rmsnorm(h, pg_ref[...]).astype(BF16), wpg_ref[...]))
    emb = _rmsnorm(_dot(p_ref[0].astype(BF16), wpp_ref[...]), ppg_ref[...])
    h = h + gate * emb
    o_ref[0] = _rmsnorm(h, fg_ref[...])


def _ffn_ple(h2, p, g, wg, wu, wd, pg, wpg, wpp, ppg, fg):
    b, s, d = h2.shape
    t = ROW_TILE
    row = pl.BlockSpec((1, t, d), lambda i, j: (i, j, 0))
    prow = pl.BlockSpec((1, t, p.shape[-1]), lambda i, j: (i, j, 0))
    weights = (g, wg, wu, wd, pg, wpg, wpp, ppg, fg)
    return pl.pallas_call(
        _ffn_ple_kernel,
        out_shape=jax.ShapeDtypeStruct((b, s, d), F32),
        grid=(b, s // t),
        in_specs=[row, prow] + [_resident(a.shape) for a in weights],
        out_specs=row,
        compiler_params=_params(2),
        name="ffn_ple",
    )(h2, p, *weights)


def _swap_halves(w):
    half = w.shape[-1] // 2
    return jnp.concatenate([w[..., half:], w[..., :half]], axis=-1)


def _pad_cols(w, width):
    return jnp.pad(w, ((0, 0), (0, width - w.shape[-1])))


def _in_weights(w_in, d, lru_w, q_lora, kv_lora):
    sizes = (lru_w, lru_w, q_lora, kv_lora, QK_ROPE, d, d)
    offs = [0]
    for sz in sizes:
        offs.append(offs[-1] + sz)
    zx, zg, cq, ckv, kr, ga, gb = (w_in[:, offs[i]:offs[i + 1]] for i in range(7))
    kr2 = jnp.concatenate([_pad_cols(kr, LANES), _pad_cols(_swap_halves(kr), LANES)], axis=1)
    parts = {"zx": zx, "zg": zg, "cq": cq, "ckv": ckv, "kr": kr2, "ga": ga, "gb": gb}
    cols, c0 = {}, 0
    for name, part in parts.items():
        cols[name] = (c0, c0 + part.shape[1])
        c0 += part.shape[1]
    return jnp.concatenate(list(parts.values()), axis=1).astype(BF16), cols


def _q_weights(w_uq):
    r = w_uq.shape[0]
    w = w_uq.reshape(r, MLA_HEADS, QK_NOPE + QK_ROPE)
    rope = w[:, :, QK_NOPE:]
    main = jnp.pad(w, ((0, 0), (0, 0), (0, QK_PAD - QK_NOPE - QK_ROPE)))
    swapped = jnp.pad(_swap_halves(rope), ((0, 0), (0, 0), (0, LANES - QK_ROPE)))
    return (main.reshape(r, MLA_HEADS * QK_PAD).astype(BF16),
            swapped.reshape(r, MLA_HEADS * LANES).astype(BF16))


def kernel(x, p, positions, ffn1_norm, ffn1_w_gate, ffn1_w_up, ffn1_w_down, mix_norm, w_in, conv_w, conv_b, lru_w_r, lru_b_r, lru_w_i, lru_b_i, lru_lambda, w_lru_out, q_norm, w_uq, kv_norm, w_ukv, w_mla_out, w_o, ffn2_norm, ffn2_w_gate, ffn2_w_up, ffn2_w_down, ple_norm, ple_w_gate, ple_w_proj, ple_proj_norm, final_norm):
    b, s, d = x.shape
    depth = ffn1_norm.shape[0]
    lru_w = conv_w.shape[-1]
    q_lora, kv_lora = q_norm.shape[-1], kv_norm.shape[-1]
    assert b == SUBLANES and lru_w == d
    assert depth == 1
    assert s % ROW_TILE == 0 and s % LRU_STEPS == 0 and s % Q_TILE == 0

    row = lambda v: v.reshape(1, -1).astype(F32)
    posf = positions.astype(F32).reshape(b, s, 1)
    half = QK_ROPE // 2
    inv_freq = ROPE_THETA ** (-jnp.arange(0, QK_ROPE, 2, dtype=F32) / QK_ROPE)
    invf = jnp.tile(inv_freq, LANES // half).reshape(1, LANES)
    sign = jnp.tile(jnp.concatenate([-jnp.ones(half, F32), jnp.ones(half, F32)]),
                    LANES // QK_ROPE).reshape(1, LANES)
    q_scale = (QK_NOPE + QK_ROPE) ** -0.5 * math.log2(math.e)

    h = x
    for i in range(depth):
        h1 = _ffn(h, row(ffn1_norm[i]), ffn1_w_gate[i].astype(BF16), ffn1_w_up[i].astype(BF16),
                  ffn1_w_down[i].astype(BF16))

        w_all, cols = _in_weights(w_in[i], d, lru_w, q_lora, kv_lora)
        wqm, wqs = _q_weights(w_uq[i])
        w_kv = w_ukv[i].reshape(kv_lora, MLA_HEADS, QK_NOPE + V_HEAD)
        wk = w_kv[:, :, :QK_NOPE].reshape(kv_lora, MLA_HEADS * QK_NOPE).astype(BF16)
        wv = w_kv[:, :, QK_NOPE:].reshape(kv_lora, MLA_HEADS * V_HEAD).astype(BF16)
        zx, zg, ga, gb, q, k, v = _inproj(h1, posf, row(mix_norm[i]), w_all, cols, row(q_norm[i]),
                                          wqm, wqs, row(kv_norm[i]), wk, wv, invf, sign, q_scale)

        zx2 = zx.reshape(s * b, lru_w)
        zg2 = zg.reshape(s * b, lru_w)
        lru_args = lambda dr: (
            conv_w[i].astype(F32), row(conv_b[i]),
            jnp.concatenate([lru_w_r[i, dr], lru_w_i[i, dr]], axis=-1).astype(BF16),
            row(lru_b_r[i, dr]), row(lru_b_i[i, dr]), row(lru_lambda[i, dr]))
        hf2 = _lru(zx2, *lru_args(0), reverse=False)
        ya2 = _lru(zx2, *lru_args(1), reverse=True, hf2=hf2, zg2=zg2)

        ob = _attn(q, k, v)

        h2 = _merge(h1, ya2.reshape(s, b * lru_w), ob, ga, gb, w_lru_out[i].astype(BF16),
                    w_mla_out[i].astype(BF16), w_o[i].astype(BF16))
        h = _ffn_ple(h2, p.reshape(b, s, -1), row(ffn2_norm[i]), ffn2_w_gate[i].astype(BF16),
                     ffn2_w_up[i].astype(BF16), ffn2_w_down[i].astype(BF16), row(ple_norm[i]),
                     ple_w_gate[i].astype(BF16), ple_w_proj[i].astype(BF16),
                     row(ple_proj_norm[i]), row(final_norm))
    return h
```

```python
import functools
import math

import jax
import jax.numpy as jnp
from jax import lax
from jax.experimental import pallas as pl
from jax.experimental.pallas import tpu as pltpu

F32 = jnp.float32
BF16 = jnp.bfloat16

LRU_BLOCKS = 8
CONV_WIDTH = 4
LRU_C = 8.0
MLA_HEADS = 8
QK_NOPE = 128
QK_ROPE = 64
V_HEAD = 128
ROPE_THETA = 10000.0
EPS = 1e-6

LANES = 128
SUBLANES = 8
BF16_ROWS = 16
QK_PAD = 256
VMEM_LIMIT_BYTES = 56 * 1024 * 1024

FFN_ROWS = 512
MERGE_ROWS = 256
STEPS = 64
Q_BLOCK = 512
FF_CHUNK = 512


def _resident(shape):
    nd = len(shape)
    return pl.BlockSpec(shape, lambda *_: (0,) * nd, pipeline_mode=pl.Buffered(1))


def _params(semantics):
    return pltpu.CompilerParams(dimension_semantics=semantics,
                                vmem_limit_bytes=VMEM_LIMIT_BYTES)


def _rmsnorm(x, g):
    ms = jnp.mean(x * x, axis=-1, keepdims=True)
    return x * lax.rsqrt(ms + EPS) * g


def _dot(a, b):
    return jnp.dot(a, b, preferred_element_type=F32)


def _dot_nt(a, b):
    return lax.dot_general(a, b, (((1,), (1,)), ((), ())), preferred_element_type=F32)


def _swiglu_half(x, g, wg_ref, wu_ref, wd_ref):
    d_ff = wg_ref.shape[1]
    xn = _rmsnorm(x, g).astype(BF16)
    y = None
    for c0 in range(0, d_ff, FF_CHUNK):
        c1 = min(c0 + FF_CHUNK, d_ff)
        gate = _dot(xn, wg_ref[:, c0:c1])
        up = _dot(xn, wu_ref[:, c0:c1])
        act = (gate * jax.nn.sigmoid(gate) * up).astype(BF16)
        part = _dot(act, wd_ref[c0:c1, :])
        y = part if y is None else y + part
    return x + 0.5 * y


def _ffn_kernel(x_ref, g_ref, wg_ref, wu_ref, wd_ref, o_ref):
    o_ref[0] = _swiglu_half(x_ref[0], g_ref[...], wg_ref, wu_ref, wd_ref)


def _ffn(x, g, wg, wu, wd):
    b, s, d = x.shape
    t = FFN_ROWS
    row = pl.BlockSpec((1, t, d), lambda i, j: (i, j, 0))
    return pl.pallas_call(
        _ffn_kernel,
        out_shape=jax.ShapeDtypeStruct((b, s, d), F32),
        grid=(b, s // t),
        in_specs=[row, _resident(g.shape), _resident(wg.shape), _resident(wu.shape),
                  _resident(wd.shape)],
        out_specs=row,
        compiler_params=_params(("parallel", "parallel")),
        name="ffn",
    )(x, g, wg, wu, wd)


def _inproj_kernel(h_ref, pos_ref, g_ref, w_ref, qg_ref, wqm_ref, wqs_ref, kvg_ref, wk_ref,
                   wv_ref, invf_ref, sign_ref,
                   zx_ref, zg_ref, ga_ref, gb_ref, q_ref, k_ref, v_ref, *, cols, q_scale):
    nb, steps, d = h_ref.shape
    rows = nb * steps
    un = _rmsnorm(h_ref[...].reshape(rows, d), g_ref[...]).astype(BF16)

    def proj(name):
        c0, c1 = cols[name]
        return _dot(un, w_ref[:, c0:c1])

    def per_batch(val):
        return val.reshape(nb, steps, val.shape[-1])

    zx = proj("zx")
    for n in range(zx_ref.shape[0]):
        for b in range(nb):
            zx_ref[n, pl.ds(b, steps, stride=nb), :] = (
                zx[b * steps:(b + 1) * steps, n * LANES:(n + 1) * LANES])
    zg_ref[...] = per_batch(proj("zg").astype(BF16))
    ga_ref[...] = per_batch(proj("ga").astype(BF16))
    gb_ref[...] = per_batch(proj("gb").astype(BF16))

    ang = pos_ref[...].reshape(rows, 1) * invf_ref[...]
    cosv = jnp.cos(ang)
    sinv = jnp.sin(ang) * sign_ref[...]

    cqn = _rmsnorm(proj("cq"), qg_ref[...]).astype(BF16)
    qm = _dot(cqn, wqm_ref[...])
    qs = _dot(cqn, wqs_ref[...])
    cq_cos = cosv * q_scale
    cq_sin = sinv * q_scale
    for h in range(MLA_HEADS):
        m0 = h * QK_PAD
        q_ref[:, h, :, 0:QK_NOPE] = per_batch((qm[:, m0:m0 + QK_NOPE] * q_scale).astype(BF16))
        rope = qm[:, m0 + QK_NOPE:m0 + QK_PAD] * cq_cos + qs[:, h * LANES:(h + 1) * LANES] * cq_sin
        q_ref[:, h, :, QK_NOPE:QK_PAD] = per_batch(rope.astype(BF16))

    ckvn = _rmsnorm(proj("ckv"), kvg_ref[...]).astype(BF16)
    kn = _dot(ckvn, wk_ref[...])
    vv = _dot(ckvn, wv_ref[...])
    kr = proj("kr")
    kpe = per_batch((kr[:, 0:LANES] * cosv + kr[:, LANES:2 * LANES] * sinv).astype(BF16))
    for h in range(MLA_HEADS):
        k_ref[:, h, :, 0:QK_NOPE] = per_batch(kn[:, h * QK_NOPE:(h + 1) * QK_NOPE].astype(BF16))
        k_ref[:, h, :, QK_NOPE:QK_PAD] = kpe
        v_ref[:, h, :, :] = per_batch(vv[:, h * V_HEAD:(h + 1) * V_HEAD].astype(BF16))


def _inproj(h1, posf, g, w, cols, qg, wqm, wqs, kvg, wk, wv, invf, sign, q_scale):
    b, s, d = h1.shape
    t = STEPS
    width = cols["zx"][1] - cols["zx"][0]
    bm = lambda x: pl.BlockSpec((b, t, x), lambda i: (0, i, 0))
    heads = lambda x: pl.BlockSpec((b, MLA_HEADS, t, x), lambda i: (0, 0, i, 0))
    kernel = functools.partial(_inproj_kernel, cols=cols, q_scale=q_scale)
    return pl.pallas_call(
        kernel,
        out_shape=[
            jax.ShapeDtypeStruct((width // LANES, s * b, LANES), F32),
            jax.ShapeDtypeStruct((b, s, width), BF16),
            jax.ShapeDtypeStruct((b, s, d), BF16),
            jax.ShapeDtypeStruct((b, s, d), BF16),
            jax.ShapeDtypeStruct((b, MLA_HEADS, s, QK_PAD), BF16),
            jax.ShapeDtypeStruct((b, MLA_HEADS, s, QK_PAD), BF16),
            jax.ShapeDtypeStruct((b, MLA_HEADS, s, V_HEAD), BF16),
        ],
        grid=(s // t,),
        in_specs=[bm(d), bm(1)] + [_resident(a.shape) for a in
                                   (g, w, qg, wqm, wqs, kvg, wk, wv, invf, sign)],
        out_specs=[pl.BlockSpec((width // LANES, t * b, LANES), lambda i: (0, i, 0)),
                   bm(width), bm(d), bm(d),
                   heads(QK_PAD), heads(QK_PAD), heads(V_HEAD)],
        compiler_params=_params(("parallel",)),
        name="inproj",
    )(h1, posf, g, w, qg, wqm, wqs, kvg, wk, wv, invf, sign)


def _lru_kernel(*refs, reverse, n_tiles, steps):
    if reverse:
        (prev_ref, cur_ref, next_ref, cw_ref, cb_ref, wg_ref, br_ref, bi_ref, lam_ref,
         hf_ref, o_ref, xpad, a_s, u_s, h_s, hb_s) = refs
    else:
        (prev_ref, cur_ref, next_ref, cw_ref, cb_ref, wg_ref, br_ref, bi_ref, lam_ref,
         o_ref, xpad, a_s, u_s, h_s) = refs
        hb_s = o_ref
    i = pl.program_id(0)
    tile = (n_tiles - 1 - i) if reverse else i
    rows = steps * SUBLANES
    halo = (CONV_WIDTH // 2) * SUBLANES
    blk = cur_ref.shape[2]

    @pl.when(i == 0)
    def _():
        h_s[...] = jnp.zeros_like(h_s)

    nlam = -lam_ref[...]
    softplus = jnp.maximum(nlam, 0.0) + jnp.log1p(jnp.exp(-jnp.abs(nlam)))
    for n in range(LRU_BLOCKS):
        c0, c1 = n * blk, (n + 1) * blk
        xpad[0:halo, :] = jnp.where(tile > 0, prev_ref[n], 0.0)
        xpad[halo:halo + rows, :] = cur_ref[n]
        xpad[halo + rows:halo + rows + SUBLANES, :] = jnp.where(
            tile < n_tiles - 1, next_ref[n, 0:SUBLANES, :], 0.0)
        xa = cb_ref[:, c0:c1] + xpad[0:rows, :] * cw_ref[0:1, c0:c1]
        for k in range(1, CONV_WIDTH):
            xa = xa + xpad[k * SUBLANES:k * SUBLANES + rows, :] * cw_ref[k:k + 1, c0:c1]

        gates = _dot(xa.astype(BF16), wg_ref[n])
        r = jax.nn.sigmoid(gates[:, 0:blk] + br_ref[:, c0:c1])
        gi = jax.nn.sigmoid(gates[:, blk:2 * blk] + bi_ref[:, c0:c1])
        log_a = (-LRU_C) * r * softplus[:, c0:c1]
        a_s[n] = jnp.exp(log_a)
        u_s[n] = jnp.sqrt(1.0 - jnp.exp(2.0 * log_a)) * (gi * xa)

    def step(j, h):
        t = (steps - 1 - j) if reverse else j
        r0 = pl.multiple_of(t * SUBLANES, SUBLANES)
        h = a_s[:, pl.ds(r0, SUBLANES), :] * h + u_s[:, pl.ds(r0, SUBLANES), :]
        hb_s[:, pl.ds(r0, SUBLANES), :] = h
        return h

    h_s[...] = lax.fori_loop(0, steps, step, h_s[...], unroll=8)

    if reverse:
        hb_s[...] = hb_s[...] + hf_ref[...]
        for n in range(LRU_BLOCKS):
            for b in range(SUBLANES):
                o_ref[b, :, n * blk:(n + 1) * blk] = (
                    hb_s[n, pl.ds(b, steps, stride=SUBLANES), :].astype(o_ref.dtype))


def _lru(zx3, cw, cb, wg, br, bi, lam, reverse, hf3=None):
    nblk, n_rows, blk = zx3.shape
    steps = STEPS
    rows = steps * SUBLANES
    n_tiles = n_rows // rows
    halo = (CONV_WIDTH // 2) * SUBLANES
    per_halo = rows // halo
    last_halo = n_rows // halo - 1

    def tile_of(i):
        return (n_tiles - 1 - i) if reverse else i

    cur = pl.BlockSpec((nblk, rows, blk), lambda i: (0, tile_of(i), 0))
    prev = pl.BlockSpec((nblk, halo, blk),
                        lambda i: (0, jnp.maximum(tile_of(i) * per_halo - 1, 0), 0))
    nxt = pl.BlockSpec((nblk, halo, blk),
                       lambda i: (0, jnp.minimum((tile_of(i) + 1) * per_halo, last_halo), 0))
    weights = (cw, cb, wg, br, bi, lam)
    in_specs = [prev, cur, nxt] + [_resident(a.shape) for a in weights]
    args = [zx3, zx3, zx3, *weights]
    scratch = [
        pltpu.VMEM((rows + halo + SUBLANES, blk), F32),
        pltpu.VMEM((nblk, rows, blk), F32),
        pltpu.VMEM((nblk, rows, blk), F32),
        pltpu.VMEM((nblk, SUBLANES, blk), F32),
    ]
    if reverse:
        in_specs.append(cur)
        args.append(hf3)
        scratch.append(pltpu.VMEM((nblk, rows, blk), F32))
        out_shape = jax.ShapeDtypeStruct((SUBLANES, n_rows // SUBLANES, nblk * blk), BF16)
        out_spec = pl.BlockSpec((SUBLANES, steps, nblk * blk), lambda i: (0, tile_of(i), 0))
    else:
        out_shape = jax.ShapeDtypeStruct((nblk, n_rows, blk), F32)
        out_spec = cur
    kernel = functools.partial(_lru_kernel, reverse=reverse, n_tiles=n_tiles, steps=steps)
    return pl.pallas_call(
        kernel,
        out_shape=out_shape,
        grid=(n_tiles,),
        in_specs=in_specs,
        out_specs=out_spec,
        scratch_shapes=scratch,
        compiler_params=_params(("arbitrary",)),
        name="lru_bwd" if reverse else "lru_fwd",
    )(*args)


def _attn_kernel(q_ref, qn_ref, k_ref, kn_ref, v_ref, o_ref, st_s, p_s, vt_s, *, steps_per_head):
    g = pl.program_id(0)
    dv = v_ref.shape[-1]
    tb = st_s.shape[-1]

    @pl.when(g == 0)
    def _():
        st_s[0] = _dot_nt(k_ref[0, 0], q_ref[0, 0, 0:tb, :])

    @pl.when(g % steps_per_head == 0)
    def _():
        vt_s[0:dv, :] = v_ref[0, 0].astype(F32).T.astype(BF16)
        row = lax.broadcasted_iota(jnp.int32, (BF16_ROWS, vt_s.shape[1]), 0)
        vt_s[dv:dv + BF16_ROWS, :] = jnp.where(row == 0, 1.0, 0.0).astype(BF16)

    def finish(slot, r0):
        st = st_s[slot]
        m = jnp.max(st, axis=0, keepdims=True)
        p_s[slot] = jnp.exp2(st - m).astype(BF16)
        ot = _dot(vt_s[...], p_s[slot])
        o = ot[0:dv, :] / ot[dv:dv + 1, :]
        o_ref[0, r0:r0 + tb, :] = o.T.astype(o_ref.dtype)

    st_s[1] = _dot_nt(k_ref[0, 0], q_ref[0, 0, tb:2 * tb, :])
    finish(0, 0)
    st_s[0] = _dot_nt(kn_ref[0, 0], qn_ref[0, 0, 0:tb, :])
    finish(1, tb)


def _attn(q, k, v):
    b, nh, s, dq = q.shape
    dv = v.shape[-1]
    tb = Q_BLOCK
    tq = 2 * tb
    nj = s // tq
    n_steps = b * nh * nj

    def bhj(g):
        return g // (nh * nj), (g // nj) % nh, g % nj

    def following(g):
        return jnp.minimum(g + 1, n_steps - 1)

    def q_map(g):
        i, h, j = bhj(g)
        return (i, h, j, 0)

    def head_map(g):
        i, h, _ = bhj(g)
        return (i, h, 0, 0)

    def o_map(g):
        i, h, j = bhj(g)
        return (i, j, h)

    kernel = functools.partial(_attn_kernel, steps_per_head=nj)
    return pl.pallas_call(
        kernel,
        out_shape=jax.ShapeDtypeStruct((b, s, nh * dv), BF16),
        grid=(n_steps,),
        in_specs=[
            pl.BlockSpec((1, 1, tq, dq), q_map),
            pl.BlockSpec((1, 1, tq, dq), lambda g: q_map(following(g))),
            pl.BlockSpec((1, 1, s, dq), head_map),
            pl.BlockSpec((1, 1, s, dq), lambda g: head_map(following(g))),
            pl.BlockSpec((1, 1, s, dv), head_map),
        ],
        out_specs=pl.BlockSpec((1, tq, dv), o_map),
        scratch_shapes=[
            pltpu.VMEM((2, s, tb), F32),
            pltpu.VMEM((2, s, tb), BF16),
            pltpu.VMEM((dv + BF16_ROWS, s), BF16),
        ],
        compiler_params=_params(("arbitrary",)),
        name="attn",
    )(q, q, k, k, v)


def _merge_kernel(h_ref, ys_ref, zg_ref, ob_ref, ga_ref, gb_ref, wl_ref, wm_ref, wo_ref, o_ref):
    ya_in = ys_ref[0].astype(F32) * jax.nn.gelu(zg_ref[0].astype(F32))
    ya = _dot(ya_in.astype(BF16), wl_ref[...])
    yb = _dot(ob_ref[0], wm_ref[...])
    merged = (jax.nn.sigmoid(ga_ref[0].astype(F32)) * ya
              + jax.nn.sigmoid(gb_ref[0].astype(F32)) * yb)
    o_ref[0] = h_ref[0] + _dot(merged.astype(BF16), wo_ref[...])


def _merge(h1, ysum, zg, ob, ga, gb, wl, wm, wo):
    b, s, d = h1.shape
    t = MERGE_ROWS
    bm = pl.BlockSpec((1, t, d), lambda i, j: (i, j, 0))
    return pl.pallas_call(
        _merge_kernel,
        out_shape=jax.ShapeDtypeStruct((b, s, d), F32),
        grid=(b, s // t),
        in_specs=[bm, bm, bm, bm, bm, bm, _resident(wl.shape), _resident(wm.shape),
                  _resident(wo.shape)],
        out_specs=bm,
        compiler_params=_params(("parallel", "parallel")),
        name="merge",
    )(h1, ysum, zg, ob, ga, gb, wl, wm, wo)


def _ffn_ple_kernel(x_ref, p_ref, g_ref, wg_ref, wu_ref, wd_ref, pg_ref, wpg_ref, wpp_ref,
                    ppg_ref, fg_ref, o_ref):
    h = _swiglu_half(x_ref[0], g_ref[...], wg_ref, wu_ref, wd_ref)
    gate = jax.nn.sigmoid(_dot(_rmsnorm(h, pg_ref[...]).astype(BF16), wpg_ref[...]))
    emb = _rmsnorm(_dot(p_ref[0].astype(BF16), wpp_ref[...]), ppg_ref[...])
    h = h + gate * emb
    o_ref[0] = _rmsnorm(h, fg_ref[...])


def _ffn_ple(h2, p, g, wg, wu, wd, pg, wpg, wpp, ppg, fg):
    b, s, d = h2.shape
    t = FFN_ROWS
    row = pl.BlockSpec((1, t, d), lambda i, j: (i, j, 0))
    prow = pl.BlockSpec((1, t, p.shape[-1]), lambda i, j: (i, j, 0))
    weights = (g, wg, wu, wd, pg, wpg, wpp, ppg, fg)
    return pl.pallas_call(
        _ffn_ple_kernel,
        out_shape=jax.ShapeDtypeStruct((b, s, d), F32),
        grid=(b, s // t),
        in_specs=[row, prow] + [_resident(a.shape) for a in weights],
        out_specs=row,
        compiler_params=_params(("parallel", "parallel")),
        name="ffn_ple",
    )(h2, p, *weights)


def _swap_halves(w):
    half = w.shape[-1] // 2
    return jnp.concatenate([w[..., half:], w[..., :half]], axis=-1)


def _pad_cols(w, width):
    return jnp.pad(w, ((0, 0), (0, width - w.shape[-1])))


def _in_weights(w_in, d, lru_w, q_lora, kv_lora):
    sizes = (lru_w, lru_w, q_lora, kv_lora, QK_ROPE, d, d)
    offs = [0]
    for sz in sizes:
        offs.append(offs[-1] + sz)
    zx, zg, cq, ckv, kr, ga, gb = (w_in[:, offs[i]:offs[i + 1]] for i in range(7))
    kr2 = jnp.concatenate([_pad_cols(kr, LANES), _pad_cols(_swap_halves(kr), LANES)], axis=1)
    parts = {"zx": zx, "zg": zg, "cq": cq, "ckv": ckv, "kr": kr2, "ga": ga, "gb": gb}
    cols, c0 = {}, 0
    for name, part in parts.items():
        cols[name] = (c0, c0 + part.shape[1])
        c0 += part.shape[1]
    return jnp.concatenate(list(parts.values()), axis=1).astype(BF16), cols


def _q_weights(w_uq):
    r = w_uq.shape[0]
    w = w_uq.reshape(r, MLA_HEADS, QK_NOPE + QK_ROPE)
    rope = w[:, :, QK_NOPE:]
    main = jnp.pad(w, ((0, 0), (0, 0), (0, QK_PAD - QK_NOPE - QK_ROPE)))
    swapped = jnp.pad(_swap_halves(rope), ((0, 0), (0, 0), (0, LANES - QK_ROPE)))
    return (main.reshape(r, MLA_HEADS * QK_PAD).astype(BF16),
            swapped.reshape(r, MLA_HEADS * LANES).astype(BF16))


def kernel(x, p, positions, ffn1_norm, ffn1_w_gate, ffn1_w_up, ffn1_w_down, mix_norm, w_in, conv_w, conv_b, lru_w_r, lru_b_r, lru_w_i, lru_b_i, lru_lambda, w_lru_out, q_norm, w_uq, kv_norm, w_ukv, w_mla_out, w_o, ffn2_norm, ffn2_w_gate, ffn2_w_up, ffn2_w_down, ple_norm, ple_w_gate, ple_w_proj, ple_proj_norm, final_norm):
    b, s, d = x.shape
    depth = ffn1_norm.shape[0]
    lru_w = conv_w.shape[-1]
    q_lora, kv_lora = q_norm.shape[-1], kv_norm.shape[-1]
    assert b == SUBLANES and lru_w == d
    assert depth == 1
    assert s % FFN_ROWS == 0 and s % MERGE_ROWS == 0 and s % STEPS == 0
    assert s % (2 * Q_BLOCK) == 0

    row = lambda v: v.reshape(1, -1).astype(F32)
    posf = positions.astype(F32).reshape(b, s, 1)
    half = QK_ROPE // 2
    inv_freq = ROPE_THETA ** (-jnp.arange(0, QK_ROPE, 2, dtype=F32) / QK_ROPE)
    invf = jnp.tile(inv_freq, LANES // half).reshape(1, LANES)
    sign = jnp.tile(jnp.concatenate([-jnp.ones(half, F32), jnp.ones(half, F32)]),
                    LANES // QK_ROPE).reshape(1, LANES)
    q_scale = (QK_NOPE + QK_ROPE) ** -0.5 * math.log2(math.e)

    i = 0
    h1 = _ffn(x, row(ffn1_norm[i]), ffn1_w_gate[i].astype(BF16), ffn1_w_up[i].astype(BF16),
              ffn1_w_down[i].astype(BF16))

    w_all, cols = _in_weights(w_in[i], d, lru_w, q_lora, kv_lora)
    wqm, wqs = _q_weights(w_uq[i])
    w_kv = w_ukv[i].reshape(kv_lora, MLA_HEADS, QK_NOPE + V_HEAD)
    wk = w_kv[:, :, :QK_NOPE].reshape(kv_lora, MLA_HEADS * QK_NOPE).astype(BF16)
    wv = w_kv[:, :, QK_NOPE:].reshape(kv_lora, MLA_HEADS * V_HEAD).astype(BF16)
    zx3, zg, ga, gb, q, k, v = _inproj(h1, posf, row(mix_norm[i]), w_all, cols, row(q_norm[i]),
                                       wqm, wqs, row(kv_norm[i]), wk, wv, invf, sign, q_scale)

    lru_args = lambda dr: (
        conv_w[i].astype(F32), row(conv_b[i]),
        jnp.concatenate([lru_w_r[i, dr], lru_w_i[i, dr]], axis=-1).astype(BF16),
        row(lru_b_r[i, dr]), row(lru_b_i[i, dr]), row(lru_lambda[i, dr]))
    hf3 = _lru(zx3, *lru_args(0), reverse=False)
    ysum = _lru(zx3, *lru_args(1), reverse=True, hf3=hf3)

    ob = _attn(q, k, v)

    h2 = _merge(h1, ysum, zg, ob, ga, gb, w_lru_out[i].astype(BF16),
                w_mla_out[i].astype(BF16), w_o[i].astype(BF16))
    return _ffn_ple(h2, p.reshape(b, s, -1), row(ffn2_norm[i]), ffn2_w_gate[i].astype(BF16),
                    ffn2_w_up[i].astype(BF16), ffn2_w_down[i].astype(BF16), row(ple_norm[i]),
                    ple_w_gate[i].astype(BF16), ple_w_proj[i].astype(BF16),
                    row(ple_proj_norm[i]), row(final_norm))
```

```python
import functools
import math

import jax
import jax.numpy as jnp
from jax import lax
from jax.experimental import pallas as pl
from jax.experimental.pallas import tpu as pltpu

F32 = jnp.float32
BF16 = jnp.bfloat16

LRU_BLOCKS = 8
CONV_WIDTH = 4
LRU_C = 8.0
MLA_HEADS = 8
QK_NOPE = 128
QK_ROPE = 64
V_HEAD = 128
ROPE_THETA = 10000.0
EPS = 1e-6

LANES = 128
SUBLANES = 8
BF16_ROWS = 16
QK_PAD = 256
VMEM_LIMIT_BYTES = 56 * 1024 * 1024

FFN_ROWS = 512
STEPS = 64
Q_BLOCK = 512
FF_CHUNK = 512


def _resident(shape):
    nd = len(shape)
    return pl.BlockSpec(shape, lambda *_: (0,) * nd, pipeline_mode=pl.Buffered(1))


def _params(semantics):
    return pltpu.CompilerParams(dimension_semantics=semantics,
                                vmem_limit_bytes=VMEM_LIMIT_BYTES)


def _rmsnorm(x, g):
    ms = jnp.mean(x * x, axis=-1, keepdims=True)
    return x * lax.rsqrt(ms + EPS) * g


def _dot(a, b):
    return jnp.dot(a, b, preferred_element_type=F32)


def _dot_nt(a, b):
    return lax.dot_general(a, b, (((1,), (1,)), ((), ())), preferred_element_type=F32)


def _swiglu_half(x, g, wg_ref, wu_ref, wd_ref):
    d_ff = wg_ref.shape[1]
    xn = _rmsnorm(x, g).astype(BF16)
    y = None
    for c0 in range(0, d_ff, FF_CHUNK):
        c1 = min(c0 + FF_CHUNK, d_ff)
        gate = _dot(xn, wg_ref[:, c0:c1])
        up = _dot(xn, wu_ref[:, c0:c1])
        act = (gate * jax.nn.sigmoid(gate) * up).astype(BF16)
        part = _dot(act, wd_ref[c0:c1, :])
        y = part if y is None else y + part
    return x + 0.5 * y


def _ffn_kernel(x_ref, g_ref, wg_ref, wu_ref, wd_ref, o_ref):
    o_ref[0] = _swiglu_half(x_ref[0], g_ref[...], wg_ref, wu_ref, wd_ref)


def _ffn(x, g, wg, wu, wd):
    b, s, d = x.shape
    t = FFN_ROWS
    row = pl.BlockSpec((1, t, d), lambda i, j: (i, j, 0))
    return pl.pallas_call(
        _ffn_kernel,
        out_shape=jax.ShapeDtypeStruct((b, s, d), F32),
        grid=(b, s // t),
        in_specs=[row, _resident(g.shape), _resident(wg.shape), _resident(wu.shape),
                  _resident(wd.shape)],
        out_specs=row,
        compiler_params=_params(("parallel", "parallel")),
        name="ffn",
    )(x, g, wg, wu, wd)


def _rope_tables(pos, invf, sign):
    rows = pos.shape[0]
    groups = LANES // (QK_ROPE // 2)
    q4 = rows // groups
    width = LANES // groups
    lane_grp = lax.broadcasted_iota(jnp.int32, (q4, LANES), 1) // width

    def by_group(vals):
        out = vals[groups - 1]
        for g in range(groups - 2, -1, -1):
            out = jnp.where(lane_grp == g, vals[g], out)
        return out

    ang = by_group([pos[g * q4:(g + 1) * q4, :] for g in range(groups)]) * invf
    tables = []
    for dense in (jnp.cos(ang), jnp.sin(ang)):
        rolled = [dense] + [pltpu.roll(dense, k * width, axis=1) for k in range(1, groups)]
        tables.append(jnp.concatenate(
            [by_group([rolled[(j - g) % groups] for j in range(groups)]) for g in range(groups)],
            axis=0))
    return tables[0], tables[1] * sign


def _inproj_kernel(h_ref, pos_ref, g_ref, w_ref, qg_ref, wqm_ref, wqs_ref, kvg_ref, wk_ref,
                   wv_ref, invf_ref, sign_ref,
                   zx_ref, zg_ref, ga_ref, gb_ref, q_ref, k_ref, v_ref, *, cols, q_scale):
    nb, steps, d = h_ref.shape
    rows = nb * steps
    un = _rmsnorm(h_ref[...].reshape(rows, d), g_ref[...]).astype(BF16)

    def proj(name):
        c0, c1 = cols[name]
        return _dot(un, w_ref[:, c0:c1])

    def per_batch(val):
        return val.reshape(nb, steps, val.shape[-1])

    zx = proj("zx")
    for n in range(zx_ref.shape[0]):
        for b in range(nb):
            zx_ref[n, pl.ds(b, steps, stride=nb), :] = (
                zx[b * steps:(b + 1) * steps, n * LANES:(n + 1) * LANES])
    zg_ref[...] = per_batch(proj("zg").astype(BF16))
    ga_ref[...] = per_batch(proj("ga").astype(BF16))
    gb_ref[...] = per_batch(proj("gb").astype(BF16))

    cosv, sinv = _rope_tables(pos_ref[...].reshape(rows, 1), invf_ref[...], sign_ref[...])

    cqn = _rmsnorm(proj("cq"), qg_ref[...]).astype(BF16)
    qm = _dot(cqn, wqm_ref[...])
    qs = _dot(cqn, wqs_ref[...])
    cq_cos = cosv * q_scale
    cq_sin = sinv * q_scale
    for h in range(MLA_HEADS):
        m0 = h * QK_PAD
        q_ref[:, h, :, 0:QK_NOPE] = per_batch((qm[:, m0:m0 + QK_NOPE] * q_scale).astype(BF16))
        rope = qm[:, m0 + QK_NOPE:m0 + QK_PAD] * cq_cos + qs[:, h * LANES:(h + 1) * LANES] * cq_sin
        q_ref[:, h, :, QK_NOPE:QK_PAD] = per_batch(rope.astype(BF16))

    ckvn = _rmsnorm(proj("ckv"), kvg_ref[...]).astype(BF16)
    kn = _dot(ckvn, wk_ref[...])
    vv = _dot(ckvn, wv_ref[...])
    kr = proj("kr")
    kpe = per_batch((kr[:, 0:LANES] * cosv + kr[:, LANES:2 * LANES] * sinv).astype(BF16))
    for h in range(MLA_HEADS):
        k_ref[:, h, :, 0:QK_NOPE] = per_batch(kn[:, h * QK_NOPE:(h + 1) * QK_NOPE].astype(BF16))
        k_ref[:, h, :, QK_NOPE:QK_PAD] = kpe
        v_ref[:, h, :, :] = per_batch(vv[:, h * V_HEAD:(h + 1) * V_HEAD].astype(BF16))


def _inproj(h1, posf, g, w, cols, qg, wqm, wqs, kvg, wk, wv, invf, sign, q_scale):
    b, s, d = h1.shape
    t = STEPS
    width = cols["zx"][1] - cols["zx"][0]
    bm = lambda x: pl.BlockSpec((b, t, x), lambda i: (0, i, 0))
    heads = lambda x: pl.BlockSpec((b, MLA_HEADS, t, x), lambda i: (0, 0, i, 0))
    kernel = functools.partial(_inproj_kernel, cols=cols, q_scale=q_scale)
    return pl.pallas_call(
        kernel,
        out_shape=[
            jax.ShapeDtypeStruct((width // LANES, s * b, LANES), F32),
            jax.ShapeDtypeStruct((b, s, width), BF16),
            jax.ShapeDtypeStruct((b, s, d), BF16),
            jax.ShapeDtypeStruct((b, s, d), BF16),
            jax.ShapeDtypeStruct((b, MLA_HEADS, s, QK_PAD), BF16),
            jax.ShapeDtypeStruct((b, MLA_HEADS, s, QK_PAD), BF16),
            jax.ShapeDtypeStruct((b, MLA_HEADS, s, V_HEAD), BF16),
        ],
        grid=(s // t,),
        in_specs=[bm(d), bm(1)] + [_resident(a.shape) for a in
                                   (g, w, qg, wqm, wqs, kvg, wk, wv, invf, sign)],
        out_specs=[pl.BlockSpec((width // LANES, t * b, LANES), lambda i: (0, i, 0)),
                   bm(width), bm(d), bm(d),
                   heads(QK_PAD), heads(QK_PAD), heads(V_HEAD)],
        compiler_params=_params(("parallel",)),
        name="inproj",
    )(h1, posf, g, w, qg, wqm, wqs, kvg, wk, wv, invf, sign)


def _lru_kernel(*refs, reverse, n_tiles, steps):
    if reverse:
        (xa_ref, wg_ref, br_ref, bi_ref, lam_ref, hf_ref, o_ref, a_s, u_s, h_s, hb_s) = refs
    else:
        (prev_ref, cur_ref, next_ref, cw_ref, cb_ref, wg_ref, br_ref, bi_ref, lam_ref,
         o_ref, xa_ref, xpad, a_s, u_s, h_s) = refs
        hb_s = o_ref
    i = pl.program_id(0)
    rows = steps * SUBLANES
    blk = xa_ref.shape[2]

    @pl.when(i == 0)
    def _():
        h_s[...] = jnp.zeros_like(h_s)

    nlam = -lam_ref[...]
    softplus = jnp.maximum(nlam, 0.0) + jnp.log1p(jnp.exp(-jnp.abs(nlam)))
    half_rate = (-0.5 * LRU_C * math.log2(math.e)) * softplus
    for n in range(LRU_BLOCKS):
        c0, c1 = n * blk, (n + 1) * blk
        if reverse:
            xa = xa_ref[n]
        else:
            tile = i
            halo = (CONV_WIDTH // 2) * SUBLANES
            xpad[0:halo, :] = jnp.where(tile > 0, prev_ref[n], 0.0)
            xpad[halo:halo + rows, :] = cur_ref[n]
            xpad[halo + rows:halo + rows + SUBLANES, :] = jnp.where(
                tile < n_tiles - 1, next_ref[n, 0:SUBLANES, :], 0.0)
            xa = cb_ref[:, c0:c1] + xpad[0:rows, :] * cw_ref[0:1, c0:c1]
            for k in range(1, CONV_WIDTH):
                xa = xa + xpad[k * SUBLANES:k * SUBLANES + rows, :] * cw_ref[k:k + 1, c0:c1]
            xa_ref[n] = xa

        gates = _dot(xa.astype(BF16), wg_ref[n])
        tr = jnp.tanh(gates[:, 0:blk] + br_ref[:, c0:c1])
        ti = jnp.tanh(gates[:, blk:2 * blk] + bi_ref[:, c0:c1])
        rate = half_rate[:, c0:c1]
        a = jnp.exp2(tr * rate + rate)
        gap = 1.0 - a * a
        root = jnp.where(gap > 0.0, gap * lax.rsqrt(gap), 0.0)
        xh = 0.5 * xa
        a_s[n] = a
        u_s[n] = root * (ti * xh + xh)

    def step(j, h):
        t = (steps - 1 - j) if reverse else j
        r0 = pl.multiple_of(t * SUBLANES, SUBLANES)
        h = a_s[:, pl.ds(r0, SUBLANES), :] * h + u_s[:, pl.ds(r0, SUBLANES), :]
        hb_s[:, pl.ds(r0, SUBLANES), :] = h
        return h

    h_s[...] = lax.fori_loop(0, steps, step, h_s[...], unroll=8)

    if reverse:
        hb_s[...] = hb_s[...] + hf_ref[...]
        for n in range(LRU_BLOCKS):
            for b in range(SUBLANES):
                o_ref[b, :, n * blk:(n + 1) * blk] = (
                    hb_s[n, pl.ds(b, steps, stride=SUBLANES), :].astype(o_ref.dtype))


def _lru(x3, conv, wg, br, bi, lam, reverse, hf3=None):
    nblk, n_rows, blk = x3.shape
    steps = STEPS
    rows = steps * SUBLANES
    n_tiles = n_rows // rows
    halo = (CONV_WIDTH // 2) * SUBLANES
    per_halo = rows // halo
    last_halo = n_rows // halo - 1

    def tile_of(i):
        return (n_tiles - 1 - i) if reverse else i

    cur = pl.BlockSpec((nblk, rows, blk), lambda i: (0, tile_of(i), 0))
    weights = (wg, br, bi, lam)
    state_scratch = [
        pltpu.VMEM((nblk, rows, blk), F32),
        pltpu.VMEM((nblk, rows, blk), F32),
        pltpu.VMEM((nblk, SUBLANES, blk), F32),
    ]
    if reverse:
        in_specs = [cur] + [_resident(a.shape) for a in weights] + [cur]
        args = [x3, *weights, hf3]
        scratch = state_scratch + [pltpu.VMEM((nblk, rows, blk), F32)]
        out_shape = jax.ShapeDtypeStruct((SUBLANES, n_rows // SUBLANES, nblk * blk), BF16)
        out_spec = pl.BlockSpec((SUBLANES, steps, nblk * blk), lambda i: (0, tile_of(i), 0))
    else:
        prev = pl.BlockSpec((nblk, halo, blk),
                            lambda i: (0, jnp.maximum(i * per_halo - 1, 0), 0))
        nxt = pl.BlockSpec((nblk, halo, blk),
                           lambda i: (0, jnp.minimum((i + 1) * per_halo, last_halo), 0))
        in_specs = [prev, cur, nxt] + [_resident(a.shape) for a in (*conv, *weights)]
        args = [x3, x3, x3, *conv, *weights]
        scratch = [pltpu.VMEM((rows + halo + SUBLANES, blk), F32)] + state_scratch
        out_shape = [jax.ShapeDtypeStruct((nblk, n_rows, blk), F32)] * 2
        out_spec = [cur, cur]
    kernel = functools.partial(_lru_kernel, reverse=reverse, n_tiles=n_tiles, steps=steps)
    return pl.pallas_call(
        kernel,
        out_shape=out_shape,
        grid=(n_tiles,),
        in_specs=in_specs,
        out_specs=out_spec,
        scratch_shapes=scratch,
        compiler_params=_params(("arbitrary",)),
        name="lru_bwd" if reverse else "lru_fwd",
    )(*args)


def _attn_kernel(q_ref, qn_ref, k_ref, kn_ref, v_ref, o_ref, st_s, p_s, vt_s, *, steps_per_head):
    g = pl.program_id(0)
    dv = v_ref.shape[-1]
    tb = st_s.shape[-1]

    @pl.when(g == 0)
    def _():
        st_s[0] = _dot_nt(k_ref[0, 0], q_ref[0, 0, 0:tb, :])

    @pl.when(g % steps_per_head == 0)
    def _():
        vt_s[0:dv, :] = v_ref[0, 0].astype(F32).T.astype(BF16)
        row = lax.broadcasted_iota(jnp.int32, (BF16_ROWS, vt_s.shape[1]), 0)
        vt_s[dv:dv + BF16_ROWS, :] = jnp.where(row == 0, 1.0, 0.0).astype(BF16)

    def finish(slot, r0):
        st = st_s[slot]
        m = jnp.max(st, axis=0, keepdims=True)
        p_s[slot] = jnp.exp2(st - m).astype(BF16)
        ot = _dot(vt_s[...], p_s[slot])
        o = ot[0:dv, :] / ot[dv:dv + 1, :]
        o_ref[0, r0:r0 + tb, :] = o.T.astype(o_ref.dtype)

    st_s[1] = _dot_nt(k_ref[0, 0], q_ref[0, 0, tb:2 * tb, :])
    finish(0, 0)
    st_s[0] = _dot_nt(kn_ref[0, 0], qn_ref[0, 0, 0:tb, :])
    finish(1, tb)


def _attn(q, k, v):
    b, nh, s, dq = q.shape
    dv = v.shape[-1]
    tb = Q_BLOCK
    tq = 2 * tb
    nj = s // tq
    n_steps = b * nh * nj

    def bhj(g):
        return g // (nh * nj), (g // nj) % nh, g % nj

    def following(g):
        return jnp.minimum(g + 1, n_steps - 1)

    def q_map(g):
        i, h, j = bhj(g)
        return (i, h, j, 0)

    def head_map(g):
        i, h, _ = bhj(g)
        return (i, h, 0, 0)

    def o_map(g):
        i, h, j = bhj(g)
        return (i, j, h)

    kernel = functools.partial(_attn_kernel, steps_per_head=nj)
    return pl.pallas_call(
        kernel,
        out_shape=jax.ShapeDtypeStruct((b, s, nh * dv), BF16),
        grid=(n_steps,),
        in_specs=[
            pl.BlockSpec((1, 1, tq, dq), q_map),
            pl.BlockSpec((1, 1, tq, dq), lambda g: q_map(following(g))),
            pl.BlockSpec((1, 1, s, dq), head_map),
            pl.BlockSpec((1, 1, s, dq), lambda g: head_map(following(g))),
            pl.BlockSpec((1, 1, s, dv), head_map),
        ],
        out_specs=pl.BlockSpec((1, tq, dv), o_map),
        scratch_shapes=[
            pltpu.VMEM((2, s, tb), F32),
            pltpu.VMEM((2, s, tb), BF16),
            pltpu.VMEM((dv + BF16_ROWS, s), BF16),
        ],
        compiler_params=_params(("arbitrary",)),
        name="attn",
    )(q, q, k, k, v)


def _tail_kernel(h_ref, ys_ref, zg_ref, ob_ref, ga_ref, gb_ref, p_ref, wl_ref, wm_ref, wo_ref,
                 g_ref, wg_ref, wu_ref, wd_ref, pg_ref, wpg_ref, wpp_ref, ppg_ref, fg_ref, o_ref):
    ya_in = ys_ref[0].astype(F32) * jax.nn.gelu(zg_ref[0].astype(F32))
    ya = _dot(ya_in.astype(BF16), wl_ref[...])
    yb = _dot(ob_ref[0], wm_ref[...])
    merged = (jax.nn.sigmoid(ga_ref[0].astype(F32)) * ya
              + jax.nn.sigmoid(gb_ref[0].astype(F32)) * yb)
    h = h_ref[0] + _dot(merged.astype(BF16), wo_ref[...])

    h = _swiglu_half(h, g_ref[...], wg_ref, wu_ref, wd_ref)
    gate = jax.nn.sigmoid(_dot(_rmsnorm(h, pg_ref[...]).astype(BF16), wpg_ref[...]))
    emb = _rmsnorm(_dot(p_ref[0].astype(BF16), wpp_ref[...]), ppg_ref[...])
    h = h + gate * emb
    o_ref[0] = _rmsnorm(h, fg_ref[...])


def _tail(h1, ysum, zg, ob, ga, gb, p, weights):
    b, s, d = h1.shape
    t = FFN_ROWS
    row = pl.BlockSpec((1, t, d), lambda i, j: (i, j, 0))
    prow = pl.BlockSpec((1, t, p.shape[-1]), lambda i, j: (i, j, 0))
    return pl.pallas_call(
        _tail_kernel,
        out_shape=jax.ShapeDtypeStruct((b, s, d), F32),
        grid=(b, s // t),
        in_specs=[row] * 6 + [prow] + [_resident(a.shape) for a in weights],
        out_specs=row,
        compiler_params=_params(("parallel", "parallel")),
        name="tail",
    )(h1, ysum, zg, ob, ga, gb, p, *weights)


def _swap_halves(w):
    half = w.shape[-1] // 2
    return jnp.concatenate([w[..., half:], w[..., :half]], axis=-1)


def _pad_cols(w, width):
    return jnp.pad(w, ((0, 0), (0, width - w.shape[-1])))


def _in_weights(w_in, d, lru_w, q_lora, kv_lora):
    sizes = (lru_w, lru_w, q_lora, kv_lora, QK_ROPE, d, d)
    offs = [0]
    for sz in sizes:
        offs.append(offs[-1] + sz)
    zx, zg, cq, ckv, kr, ga, gb = (w_in[:, offs[i]:offs[i + 1]] for i in range(7))
    kr2 = jnp.concatenate([_pad_cols(kr, LANES), _pad_cols(_swap_halves(kr), LANES)], axis=1)
    parts = {"zx": zx, "zg": zg, "cq": cq, "ckv": ckv, "kr": kr2, "ga": ga, "gb": gb}
    cols, c0 = {}, 0
    for name, part in parts.items():
        cols[name] = (c0, c0 + part.shape[1])
        c0 += part.shape[1]
    return jnp.concatenate(list(parts.values()), axis=1).astype(BF16), cols


def _q_weights(w_uq):
    r = w_uq.shape[0]
    w = w_uq.reshape(r, MLA_HEADS, QK_NOPE + QK_ROPE)
    rope = w[:, :, QK_NOPE:]
    main = jnp.pad(w, ((0, 0), (0, 0), (0, QK_PAD - QK_NOPE - QK_ROPE)))
    swapped = jnp.pad(_swap_halves(rope), ((0, 0), (0, 0), (0, LANES - QK_ROPE)))
    return (main.reshape(r, MLA_HEADS * QK_PAD).astype(BF16),
            swapped.reshape(r, MLA_HEADS * LANES).astype(BF16))


def kernel(x, p, positions, ffn1_norm, ffn1_w_gate, ffn1_w_up, ffn1_w_down, mix_norm, w_in, conv_w, conv_b, lru_w_r, lru_b_r, lru_w_i, lru_b_i, lru_lambda, w_lru_out, q_norm, w_uq, kv_norm, w_ukv, w_mla_out, w_o, ffn2_norm, ffn2_w_gate, ffn2_w_up, ffn2_w_down, ple_norm, ple_w_gate, ple_w_proj, ple_proj_norm, final_norm):
    b, s, d = x.shape
    depth = ffn1_norm.shape[0]
    lru_w = conv_w.shape[-1]
    q_lora, kv_lora = q_norm.shape[-1], kv_norm.shape[-1]
    assert b == SUBLANES and lru_w == d
    assert depth == 1
    assert s % FFN_ROWS == 0 and s % STEPS == 0
    assert s % (2 * Q_BLOCK) == 0

    row = lambda v: v.reshape(1, -1).astype(F32)
    posf = positions.astype(F32).reshape(b, s, 1)
    half = QK_ROPE // 2
    inv_freq = ROPE_THETA ** (-jnp.arange(0, QK_ROPE, 2, dtype=F32) / QK_ROPE)
    invf = jnp.tile(inv_freq, LANES // half).reshape(1, LANES)
    sign = jnp.tile(jnp.concatenate([-jnp.ones(half, F32), jnp.ones(half, F32)]),
                    LANES // QK_ROPE).reshape(1, LANES)
    q_scale = (QK_NOPE + QK_ROPE) ** -0.5 * math.log2(math.e)

    i = 0
    h1 = _ffn(x, row(ffn1_norm[i]), ffn1_w_gate[i].astype(BF16), ffn1_w_up[i].astype(BF16),
              ffn1_w_down[i].astype(BF16))

    w_all, cols = _in_weights(w_in[i], d, lru_w, q_lora, kv_lora)
    wqm, wqs = _q_weights(w_uq[i])
    w_kv = w_ukv[i].reshape(kv_lora, MLA_HEADS, QK_NOPE + V_HEAD)
    wk = w_kv[:, :, :QK_NOPE].reshape(kv_lora, MLA_HEADS * QK_NOPE).astype(BF16)
    wv = w_kv[:, :, QK_NOPE:].reshape(kv_lora, MLA_HEADS * V_HEAD).astype(BF16)
    zx3, zg, ga, gb, q, k, v = _inproj(h1, posf, row(mix_norm[i]), w_all, cols, row(q_norm[i]),
                                       wqm, wqs, row(kv_norm[i]), wk, wv, invf, sign, q_scale)

    lru_args = lambda dr: (
        (0.5 * jnp.concatenate([lru_w_r[i, dr], lru_w_i[i, dr]], axis=-1)).astype(BF16),
        row(0.5 * lru_b_r[i, dr]), row(0.5 * lru_b_i[i, dr]), row(lru_lambda[i, dr]))
    conv = (conv_w[i].astype(F32), row(conv_b[i]))
    hf3, xa3 = _lru(zx3, conv, *lru_args(0), reverse=False)
    ysum = _lru(xa3, None, *lru_args(1), reverse=True, hf3=hf3)

    ob = _attn(q, k, v)

    tail_weights = (
        w_lru_out[i].astype(BF16), w_mla_out[i].astype(BF16), w_o[i].astype(BF16),
        row(ffn2_norm[i]), ffn2_w_gate[i].astype(BF16), ffn2_w_up[i].astype(BF16),
        ffn2_w_down[i].astype(BF16), row(ple_norm[i]), ple_w_gate[i].astype(BF16),
        ple_w_proj[i].astype(BF16), row(ple_proj_norm[i]), row(final_norm))
    return _tail(h1, ysum, zg, ob, ga, gb, p.reshape(b, s, -1), tail_weights)
```

```python
import functools
import math

import jax
import jax.numpy as jnp
from jax import lax
from jax.experimental import pallas as pl
from jax.experimental.pallas import tpu as pltpu

F32 = jnp.float32
BF16 = jnp.bfloat16

LRU_BLOCKS = 8
CONV_WIDTH = 4
LRU_C = 8.0
MLA_HEADS = 8
QK_NOPE = 128
QK_ROPE = 64
V_HEAD = 128
ROPE_THETA = 10000.0
EPS = 1e-6

LANES = 128
SUBLANES = 8
BF16_ROWS = 16
QK_PAD = 256
VMEM_LIMIT_BYTES = 56 * 1024 * 1024

FFN_ROWS = 512
STEPS = 64
Q_BLOCK = 512
FF_CHUNK = 512


def _resident(shape):
    nd = len(shape)
    return pl.BlockSpec(shape, lambda *_: (0,) * nd, pipeline_mode=pl.Buffered(1))


def _params(semantics):
    return pltpu.CompilerParams(dimension_semantics=semantics,
                                vmem_limit_bytes=VMEM_LIMIT_BYTES)


def _rmsnorm(x, g):
    ms = jnp.mean(x * x, axis=-1, keepdims=True)
    return x * lax.rsqrt(ms + EPS) * g


def _dot(a, b):
    return jnp.dot(a, b, preferred_element_type=F32)


def _dot_nt(a, b):
    return lax.dot_general(a, b, (((1,), (1,)), ((), ())), preferred_element_type=F32)


def _swiglu_half(x, g, wg_ref, wu_ref, wd_ref):
    d_ff = wg_ref.shape[1]
    xn = _rmsnorm(x, g).astype(BF16)
    y = None
    for c0 in range(0, d_ff, FF_CHUNK):
        c1 = min(c0 + FF_CHUNK, d_ff)
        gate = _dot(xn, wg_ref[:, c0:c1])
        up = _dot(xn, wu_ref[:, c0:c1])
        act = (gate * jax.nn.sigmoid(gate) * up).astype(BF16)
        part = _dot(act, wd_ref[c0:c1, :])
        y = part if y is None else y + part
    return x + 0.5 * y


def _ffn_kernel(x_ref, g_ref, wg_ref, wu_ref, wd_ref, o_ref):
    o_ref[0] = _swiglu_half(x_ref[0], g_ref[...], wg_ref, wu_ref, wd_ref)


def _ffn(x, g, wg, wu, wd):
    b, s, d = x.shape
    t = FFN_ROWS
    row = pl.BlockSpec((1, t, d), lambda i, j: (i, j, 0))
    return pl.pallas_call(
        _ffn_kernel,
        out_shape=jax.ShapeDtypeStruct((b, s, d), F32),
        grid=(b, s // t),
        in_specs=[row, _resident(g.shape), _resident(wg.shape), _resident(wu.shape),
                  _resident(wd.shape)],
        out_specs=row,
        compiler_params=_params(("parallel", "parallel")),
        name="ffn",
    )(x, g, wg, wu, wd)


def _rope_tables(pos, invf, sign):
    rows = pos.shape[0]
    groups = LANES // (QK_ROPE // 2)
    q4 = rows // groups
    width = LANES // groups
    lane_grp = lax.broadcasted_iota(jnp.int32, (q4, LANES), 1) // width

    def by_group(vals):
        out = vals[groups - 1]
        for g in range(groups - 2, -1, -1):
            out = jnp.where(lane_grp == g, vals[g], out)
        return out

    ang = by_group([pos[g * q4:(g + 1) * q4, :] for g in range(groups)]) * invf
    tables = []
    for dense in (jnp.cos(ang), jnp.sin(ang)):
        rolled = [dense] + [pltpu.roll(dense, k * width, axis=1) for k in range(1, groups)]
        tables.append(jnp.concatenate(
            [by_group([rolled[(j - g) % groups] for j in range(groups)]) for g in range(groups)],
            axis=0))
    return tables[0], tables[1] * sign


def _lru_rates(lam):
    nlam = -lam
    softplus = jnp.maximum(nlam, 0.0) + jnp.log1p(jnp.exp(-jnp.abs(nlam)))
    return (-0.5 * LRU_C * math.log2(math.e)) * softplus


def _lru_gate_block(xa, n, rate, wg_ref, br_ref, bi_ref):
    blk = xa.shape[1]
    c0, c1 = n * blk, (n + 1) * blk
    gates = _dot(xa.astype(BF16), wg_ref[n])
    tr = jnp.tanh(gates[:, 0:blk] + br_ref[:, c0:c1])
    ti = jnp.tanh(gates[:, blk:2 * blk] + bi_ref[:, c0:c1])
    a = jnp.exp2(tr * rate[:, c0:c1] + rate[:, c0:c1])
    gap = 1.0 - a * a
    root = jnp.where(gap > 0.0, gap * lax.rsqrt(gap), 0.0)
    xh = 0.5 * xa
    return a, root * (ti * xh + xh)


def _lru_scan(a_s, u_s, h_s, out_ref, steps, reverse):
    def step(j, h):
        t = (steps - 1 - j) if reverse else j
        r0 = pl.multiple_of(t * SUBLANES, SUBLANES)
        h = a_s[:, pl.ds(r0, SUBLANES), :] * h + u_s[:, pl.ds(r0, SUBLANES), :]
        out_ref[:, pl.ds(r0, SUBLANES), :] = h
        return h

    h_s[...] = lax.fori_loop(0, steps, step, h_s[...], unroll=8)


def _inproj_kernel(h_ref, hp_ref, hn_ref, pos_ref, g_ref, w_ref, qg_ref, wqm_ref, kvg_ref,
                   wk_ref, wv_ref, invf_ref, sign_ref, cw_ref, cb_ref, wgl_ref, br_ref, bi_ref,
                   lam_ref,
                   zg_ref, ga_ref, gb_ref, q_ref, k_ref, v_ref, hf_ref, xa_ref,
                   xpad, a_s, u_s, h_s, hf_s, *, cols, q_scale):
    i = pl.program_id(0)
    nb, steps, d = h_ref.shape
    rows = nb * steps
    halo = hp_ref.shape[1]
    un = _rmsnorm(h_ref[...].reshape(rows, d), g_ref[...]).astype(BF16)

    def proj(name):
        c0, c1 = cols[name]
        return _dot(un, w_ref[:, c0:c1])

    def per_batch(val):
        return val.reshape(nb, steps, val.shape[-1])

    @pl.when(i == 0)
    def _():
        h_s[...] = jnp.zeros_like(h_s)

    zx = proj("zx")
    h_halo = jnp.concatenate([hp_ref[...].reshape(nb * halo, d), hn_ref[...].reshape(nb * halo, d)])
    c0, c1 = cols["zx"]
    zx_halo = _dot(_rmsnorm(h_halo, g_ref[...]).astype(BF16), w_ref[:, c0:c1])
    zx_prev = jnp.where(i > 0, zx_halo[0:nb * halo], 0.0)
    zx_next = jnp.where(i < pl.num_programs(0) - 1, zx_halo[nb * halo:], 0.0)
    first = halo * nb
    for n in range(LRU_BLOCKS):
        lanes = slice(n * LANES, (n + 1) * LANES)
        for b in range(nb):
            xpad[n, pl.ds(b, halo, stride=nb), :] = zx_prev[b * halo:(b + 1) * halo, lanes]
            xpad[n, pl.ds(first + b, steps, stride=nb), :] = zx[b * steps:(b + 1) * steps, lanes]
            xpad[n, pl.ds(first + rows + b, halo, stride=nb), :] = (
                zx_next[b * halo:(b + 1) * halo, lanes])

    rate = _lru_rates(lam_ref[...])
    for n in range(LRU_BLOCKS):
        lanes = slice(n * LANES, (n + 1) * LANES)
        xa = cb_ref[:, lanes]
        for k in range(CONV_WIDTH):
            r0 = first + (k - CONV_WIDTH // 2) * SUBLANES
            xa = xa + xpad[n, r0:r0 + rows, :] * cw_ref[k:k + 1, lanes]
        xa_ref[n] = xa.astype(xa_ref.dtype)
        a_s[n], u_s[n] = _lru_gate_block(xa, n, rate, wgl_ref, br_ref, bi_ref)

    zg_ref[...] = per_batch(proj("zg").astype(BF16))
    ga_ref[...] = per_batch(proj("ga").astype(BF16))
    gb_ref[...] = per_batch(proj("gb").astype(BF16))

    cosv, sinv = _rope_tables(pos_ref[...].reshape(rows, 1), invf_ref[...], sign_ref[...])
    lane = lax.broadcasted_iota(jnp.int32, (rows, LANES), 1)
    low = lane < QK_ROPE
    table = jnp.where(low, cosv, sinv)

    def rotate(x, tab):
        y = x * tab
        return jnp.where(low, y + pltpu.roll(y, QK_ROPE, axis=1), 0.0)

    cqn = _rmsnorm(proj("cq"), qg_ref[...]).astype(BF16)
    qm = _dot(cqn, wqm_ref[...])
    q_table = table * q_scale
    for h in range(MLA_HEADS):
        m0 = h * QK_PAD
        q_ref[:, h, :, 0:QK_NOPE] = per_batch((qm[:, m0:m0 + QK_NOPE] * q_scale).astype(BF16))
        rope = rotate(qm[:, m0 + QK_NOPE:m0 + QK_PAD], q_table)
        q_ref[:, h, :, QK_NOPE:QK_PAD] = per_batch(rope.astype(BF16))

    ckvn = _rmsnorm(proj("ckv"), kvg_ref[...]).astype(BF16)
    kn = _dot(ckvn, wk_ref[...])
    vv = _dot(ckvn, wv_ref[...])
    kpe = per_batch(rotate(proj("kr"), table).astype(BF16))
    for h in range(MLA_HEADS):
        k_ref[:, h, :, 0:QK_NOPE] = per_batch(kn[:, h * QK_NOPE:(h + 1) * QK_NOPE].astype(BF16))
        k_ref[:, h, :, QK_NOPE:QK_PAD] = kpe
        v_ref[:, h, :, :] = per_batch(vv[:, h * V_HEAD:(h + 1) * V_HEAD].astype(BF16))

    _lru_scan(a_s, u_s, h_s, hf_s, steps, reverse=False)
    hf_ref[...] = hf_s[...].astype(hf_ref.dtype)


def _inproj(h1, posf, g, w, cols, qg, wqm, kvg, wk, wv, invf, sign, q_scale, lru_weights):
    b, s, d = h1.shape
    t = STEPS
    halo = SUBLANES
    width = cols["zx"][1] - cols["zx"][0]
    nblk = width // LANES
    rows = t * b
    bm = lambda x: pl.BlockSpec((b, t, x), lambda i: (0, i, 0))
    heads = lambda x: pl.BlockSpec((b, MLA_HEADS, t, x), lambda i: (0, 0, i, 0))
    tm = pl.BlockSpec((nblk, rows, LANES), lambda i: (0, i, 0))
    per_halo = t // halo
    prev = pl.BlockSpec((b, halo, d), lambda i: (0, jnp.maximum(i * per_halo - 1, 0), 0))
    nxt = pl.BlockSpec((b, halo, d), lambda i: (0, jnp.minimum((i + 1) * per_halo,
                                                               s // halo - 1), 0))
    weights = (g, w, qg, wqm, kvg, wk, wv, invf, sign, *lru_weights)
    kernel = functools.partial(_inproj_kernel, cols=cols, q_scale=q_scale)
    return pl.pallas_call(
        kernel,
        out_shape=[
            jax.ShapeDtypeStruct((b, s, width), BF16),
            jax.ShapeDtypeStruct((b, s, d), BF16),
            jax.ShapeDtypeStruct((b, s, d), BF16),
            jax.ShapeDtypeStruct((b, MLA_HEADS, s, QK_PAD), BF16),
            jax.ShapeDtypeStruct((b, MLA_HEADS, s, QK_PAD), BF16),
            jax.ShapeDtypeStruct((b, MLA_HEADS, s, V_HEAD), BF16),
            jax.ShapeDtypeStruct((nblk, s * b, LANES), BF16),
            jax.ShapeDtypeStruct((nblk, s * b, LANES), BF16),
        ],
        grid=(s // t,),
        in_specs=[bm(d), prev, nxt, bm(1)] + [_resident(a.shape) for a in weights],
        out_specs=[bm(width), bm(d), bm(d), heads(QK_PAD), heads(QK_PAD), heads(V_HEAD), tm, tm],
        scratch_shapes=[
            pltpu.VMEM((nblk, rows + 2 * halo * b, LANES), F32),
            pltpu.VMEM((nblk, rows, LANES), F32),
            pltpu.VMEM((nblk, rows, LANES), F32),
            pltpu.VMEM((nblk, SUBLANES, LANES), F32),
            pltpu.VMEM((nblk, rows, LANES), F32),
        ],
        compiler_params=_params(("arbitrary",)),
        name="inproj",
    )(h1, h1, h1, posf, *weights)


def _lru_bwd_kernel(xa_ref, hf_ref, wg_ref, br_ref, bi_ref, lam_ref, o_ref, a_s, u_s, h_s, hb_s,
                    *, steps):
    @pl.when(pl.program_id(0) == 0)
    def _():
        h_s[...] = jnp.zeros_like(h_s)

    rate = _lru_rates(lam_ref[...])
    for n in range(LRU_BLOCKS):
        a_s[n], u_s[n] = _lru_gate_block(xa_ref[n].astype(F32), n, rate, wg_ref, br_ref, bi_ref)
    _lru_scan(a_s, u_s, h_s, hb_s, steps, reverse=True)

    hb_s[...] = hb_s[...] + hf_ref[...].astype(F32)
    blk = xa_ref.shape[2]
    for n in range(LRU_BLOCKS):
        for b in range(SUBLANES):
            o_ref[b, :, n * blk:(n + 1) * blk] = (
                hb_s[n, pl.ds(b, steps, stride=SUBLANES), :].astype(o_ref.dtype))


def _lru_bwd(xa3, hf3, wg, br, bi, lam):
    nblk, n_rows, blk = xa3.shape
    steps = STEPS
    rows = steps * SUBLANES
    n_tiles = n_rows // rows
    cur = pl.BlockSpec((nblk, rows, blk), lambda i: (0, n_tiles - 1 - i, 0))
    weights = (wg, br, bi, lam)
    return pl.pallas_call(
        functools.partial(_lru_bwd_kernel, steps=steps),
        out_shape=jax.ShapeDtypeStruct((SUBLANES, n_rows // SUBLANES, nblk * blk), BF16),
        grid=(n_tiles,),
        in_specs=[cur, cur] + [_resident(a.shape) for a in weights],
        out_specs=pl.BlockSpec((SUBLANES, steps, nblk * blk), lambda i: (0, n_tiles - 1 - i, 0)),
        scratch_shapes=[
            pltpu.VMEM((nblk, rows, blk), F32),
            pltpu.VMEM((nblk, rows, blk), F32),
            pltpu.VMEM((nblk, SUBLANES, blk), F32),
            pltpu.VMEM((nblk, rows, blk), F32),
        ],
        compiler_params=_params(("arbitrary",)),
        name="lru_bwd",
    )(xa3, hf3, *weights)


def _attn_kernel(q_ref, qn_ref, k_ref, kn_ref, v_ref, o_ref, st_s, p_s, vt_s, *, steps_per_head):
    g = pl.program_id(0)
    dv = v_ref.shape[-1]
    tb = st_s.shape[-1]

    @pl.when(g == 0)
    def _():
        st_s[0] = _dot_nt(k_ref[0, 0], q_ref[0, 0, 0:tb, :])

    @pl.when(g % steps_per_head == 0)
    def _():
        vt_s[0:dv, :] = v_ref[0, 0].astype(F32).T.astype(BF16)
        row = lax.broadcasted_iota(jnp.int32, (BF16_ROWS, vt_s.shape[1]), 0)
        vt_s[dv:dv + BF16_ROWS, :] = jnp.where(row == 0, 1.0, 0.0).astype(BF16)

    def finish(slot, r0):
        st = st_s[slot]
        m = jnp.max(st, axis=0, keepdims=True)
        p_s[slot] = jnp.exp2(st - m).astype(BF16)
        ot = _dot(vt_s[...], p_s[slot])
        o = ot[0:dv, :] / ot[dv:dv + 1, :]
        o_ref[0, r0:r0 + tb, :] = o.T.astype(o_ref.dtype)

    st_s[1] = _dot_nt(k_ref[0, 0], q_ref[0, 0, tb:2 * tb, :])
    finish(0, 0)
    st_s[0] = _dot_nt(kn_ref[0, 0], qn_ref[0, 0, 0:tb, :])
    finish(1, tb)


def _attn(q, k, v):
    b, nh, s, dq = q.shape
    dv = v.shape[-1]
    tb = Q_BLOCK
    tq = 2 * tb
    nj = s // tq
    n_steps = b * nh * nj

    def bhj(g):
        return g // (nh * nj), (g // nj) % nh, g % nj

    def following(g):
        return jnp.minimum(g + 1, n_steps - 1)

    def q_map(g):
        i, h, j = bhj(g)
        return (i, h, j, 0)

    def head_map(g):
        i, h, _ = bhj(g)
        return (i, h, 0, 0)

    def o_map(g):
        i, h, j = bhj(g)
        return (i, j, h)

    kernel = functools.partial(_attn_kernel, steps_per_head=nj)
    return pl.pallas_call(
        kernel,
        out_shape=jax.ShapeDtypeStruct((b, s, nh * dv), BF16),
        grid=(n_steps,),
        in_specs=[
            pl.BlockSpec((1, 1, tq, dq), q_map),
            pl.BlockSpec((1, 1, tq, dq), lambda g: q_map(following(g))),
            pl.BlockSpec((1, 1, s, dq), head_map),
            pl.BlockSpec((1, 1, s, dq), lambda g: head_map(following(g))),
            pl.BlockSpec((1, 1, s, dv), head_map),
        ],
        out_specs=pl.BlockSpec((1, tq, dv), o_map),
        scratch_shapes=[
            pltpu.VMEM((2, s, tb), F32),
            pltpu.VMEM((2, s, tb), BF16),
            pltpu.VMEM((dv + BF16_ROWS, s), BF16),
        ],
        compiler_params=_params(("arbitrary",)),
        name="attn",
    )(q, q, k, k, v)


def _tail_kernel(h_ref, ys_ref, zg_ref, ob_ref, ga_ref, gb_ref, p_ref, wl_ref, wm_ref, wo_ref,
                 g_ref, wg_ref, wu_ref, wd_ref, pg_ref, wpg_ref, wpp_ref, ppg_ref, fg_ref, o_ref):
    ya_in = ys_ref[0].astype(F32) * jax.nn.gelu(zg_ref[0].astype(F32))
    ya = _dot(ya_in.astype(BF16), wl_ref[...])
    yb = _dot(ob_ref[0], wm_ref[...])
    merged = (jax.nn.sigmoid(ga_ref[0].astype(F32)) * ya
              + jax.nn.sigmoid(gb_ref[0].astype(F32)) * yb)
    h = h_ref[0] + _dot(merged.astype(BF16), wo_ref[...])

    h = _swiglu_half(h, g_ref[...], wg_ref, wu_ref, wd_ref)
    gate = jax.nn.sigmoid(_dot(_rmsnorm(h, pg_ref[...]).astype(BF16), wpg_ref[...]))
    emb = _rmsnorm(_dot(p_ref[0].astype(BF16), wpp_ref[...]), ppg_ref[...])
    h = h + gate * emb
    o_ref[0] = _rmsnorm(h, fg_ref[...])


def _tail(h1, ysum, zg, ob, ga, gb, p, weights):
    b, s, d = h1.shape
    t = FFN_ROWS
    row = pl.BlockSpec((1, t, d), lambda i, j: (i, j, 0))
    prow = pl.BlockSpec((1, t, p.shape[-1]), lambda i, j: (i, j, 0))
    return pl.pallas_call(
        _tail_kernel,
        out_shape=jax.ShapeDtypeStruct((b, s, d), F32),
        grid=(b, s // t),
        in_specs=[row] * 6 + [prow] + [_resident(a.shape) for a in weights],
        out_specs=row,
        compiler_params=_params(("parallel", "parallel")),
        name="tail",
    )(h1, ysum, zg, ob, ga, gb, p, *weights)


def _swap_halves(w):
    half = w.shape[-1] // 2
    return jnp.concatenate([w[..., half:], w[..., :half]], axis=-1)


def _in_weights(w_in, d, lru_w, q_lora, kv_lora):
    sizes = (lru_w, lru_w, q_lora, kv_lora, QK_ROPE, d, d)
    offs = [0]
    for sz in sizes:
        offs.append(offs[-1] + sz)
    zx, zg, cq, ckv, kr, ga, gb = (w_in[:, offs[i]:offs[i + 1]] for i in range(7))
    kr2 = jnp.concatenate([kr, _swap_halves(kr)], axis=1)
    parts = {"zx": zx, "zg": zg, "cq": cq, "ckv": ckv, "kr": kr2, "ga": ga, "gb": gb}
    cols, c0 = {}, 0
    for name, part in parts.items():
        cols[name] = (c0, c0 + part.shape[1])
        c0 += part.shape[1]
    return jnp.concatenate(list(parts.values()), axis=1).astype(BF16), cols


def _q_weights(w_uq):
    r = w_uq.shape[0]
    w = w_uq.reshape(r, MLA_HEADS, QK_NOPE + QK_ROPE)
    full = jnp.concatenate([w, _swap_halves(w[:, :, QK_NOPE:])], axis=-1)
    assert full.shape[-1] == QK_PAD
    return full.reshape(r, MLA_HEADS * QK_PAD).astype(BF16)


def kernel(x, p, positions, ffn1_norm, ffn1_w_gate, ffn1_w_up, ffn1_w_down, mix_norm, w_in, conv_w, conv_b, lru_w_r, lru_b_r, lru_w_i, lru_b_i, lru_lambda, w_lru_out, q_norm, w_uq, kv_norm, w_ukv, w_mla_out, w_o, ffn2_norm, ffn2_w_gate, ffn2_w_up, ffn2_w_down, ple_norm, ple_w_gate, ple_w_proj, ple_proj_norm, final_norm):
    b, s, d = x.shape
    depth = ffn1_norm.shape[0]
    lru_w = conv_w.shape[-1]
    q_lora, kv_lora = q_norm.shape[-1], kv_norm.shape[-1]
    assert b == SUBLANES and lru_w == d
    assert depth == 1
    assert s % FFN_ROWS == 0 and s % STEPS == 0
    assert s % (2 * Q_BLOCK) == 0

    row = lambda v: v.reshape(1, -1).astype(F32)
    posf = positions.astype(F32).reshape(b, s, 1)
    half = QK_ROPE // 2
    inv_freq = ROPE_THETA ** (-jnp.arange(0, QK_ROPE, 2, dtype=F32) / QK_ROPE)
    invf = jnp.tile(inv_freq, LANES // half).reshape(1, LANES)
    sign = jnp.tile(jnp.concatenate([-jnp.ones(half, F32), jnp.ones(half, F32)]),
                    LANES // QK_ROPE).reshape(1, LANES)
    q_scale = (QK_NOPE + QK_ROPE) ** -0.5 * math.log2(math.e)

    i = 0
    h1 = _ffn(x, row(ffn1_norm[i]), ffn1_w_gate[i].astype(BF16), ffn1_w_up[i].astype(BF16),
              ffn1_w_down[i].astype(BF16))

    w_all, cols = _in_weights(w_in[i], d, lru_w, q_lora, kv_lora)
    wqm = _q_weights(w_uq[i])
    w_kv = w_ukv[i].reshape(kv_lora, MLA_HEADS, QK_NOPE + V_HEAD)
    wk = w_kv[:, :, :QK_NOPE].reshape(kv_lora, MLA_HEADS * QK_NOPE).astype(BF16)
    wv = w_kv[:, :, QK_NOPE:].reshape(kv_lora, MLA_HEADS * V_HEAD).astype(BF16)
    lru_args = lambda dr: (
        (0.5 * jnp.concatenate([lru_w_r[i, dr], lru_w_i[i, dr]], axis=-1)).astype(BF16),
        row(0.5 * lru_b_r[i, dr]), row(0.5 * lru_b_i[i, dr]), row(lru_lambda[i, dr]))
    conv = (conv_w[i].astype(F32), row(conv_b[i]))
    zg, ga, gb, q, k, v, hf3, xa3 = _inproj(
        h1, posf, row(mix_norm[i]), w_all, cols, row(q_norm[i]), wqm, row(kv_norm[i]),
        wk, wv, invf, sign, q_scale, (*conv, *lru_args(0)))
    ysum = _lru_bwd(xa3, hf3, *lru_args(1))

    ob = _attn(q, k, v)

    tail_weights = (
        w_lru_out[i].astype(BF16), w_mla_out[i].astype(BF16), w_o[i].astype(BF16),
        row(ffn2_norm[i]), ffn2_w_gate[i].astype(BF16), ffn2_w_up[i].astype(BF16),
        ffn2_w_down[i].astype(BF16), row(ple_norm[i]), ple_w_gate[i].astype(BF16),
        ple_w_proj[i].astype(BF16), row(ple_proj_norm[i]), row(final_norm))
    return _tail(h1, ysum, zg, ob, ga, gb, p.reshape(b, s, -1), tail_weights)
```

```python
import functools
import math

import jax
import jax.numpy as jnp
from jax import lax
from jax.experimental import pallas as pl
from jax.experimental.pallas import tpu as pltpu

F32 = jnp.float32
BF16 = jnp.bfloat16

LRU_BLOCKS = 8
CONV_WIDTH = 4
LRU_C = 8.0
MLA_HEADS = 8
QK_NOPE = 128
QK_ROPE = 64
V_HEAD = 128
ROPE_THETA = 10000.0
EPS = 1e-6

LANES = 128
SUBLANES = 8
BF16_ROWS = 16
QK_PAD = 256
STAB_LANE = QK_NOPE + QK_ROPE
DENOM_FLOOR = 2.0 ** -64
VMEM_LIMIT_BYTES = 56 * 1024 * 1024

FFN_ROWS = 512
STEPS = 64
Q_BLOCK = 512
FF_CHUNK = 512


def _resident(shape):
    nd = len(shape)
    return pl.BlockSpec(shape, lambda *_: (0,) * nd, pipeline_mode=pl.Buffered(1))


def _params(semantics):
    return pltpu.CompilerParams(dimension_semantics=semantics,
                                vmem_limit_bytes=VMEM_LIMIT_BYTES)


def _rmsnorm(x, g):
    ms = jnp.mean(x * x, axis=-1, keepdims=True)
    return x * lax.rsqrt(ms + EPS) * g


def _dot(a, b):
    return jnp.dot(a, b, preferred_element_type=F32)


def _dot_nt(a, b):
    return lax.dot_general(a, b, (((1,), (1,)), ((), ())), preferred_element_type=F32)


def _swiglu_half(x, g, wg_ref, wu_ref, wd_ref):
    d_ff = wg_ref.shape[1]
    xn = _rmsnorm(x, g).astype(BF16)
    y = None
    for c0 in range(0, d_ff, FF_CHUNK):
        c1 = min(c0 + FF_CHUNK, d_ff)
        gate = _dot(xn, wg_ref[:, c0:c1])
        up = _dot(xn, wu_ref[:, c0:c1])
        act = (gate * jax.nn.sigmoid(gate) * up).astype(BF16)
        part = _dot(act, wd_ref[c0:c1, :])
        y = part if y is None else y + part
    return x + 0.5 * y


def _ffn_kernel(x_ref, g_ref, wg_ref, wu_ref, wd_ref, o_ref):
    o_ref[0] = _swiglu_half(x_ref[0], g_ref[...], wg_ref, wu_ref, wd_ref)


def _ffn(x, g, wg, wu, wd):
    b, s, d = x.shape
    t = FFN_ROWS
    row = pl.BlockSpec((1, t, d), lambda i, j: (i, j, 0))
    return pl.pallas_call(
        _ffn_kernel,
        out_shape=jax.ShapeDtypeStruct((b, s, d), F32),
        grid=(b, s // t),
        in_specs=[row, _resident(g.shape), _resident(wg.shape), _resident(wu.shape),
                  _resident(wd.shape)],
        out_specs=row,
        compiler_params=_params(("parallel", "parallel")),
        name="ffn",
    )(x, g, wg, wu, wd)


def _rope_tables(pos, invf, sign):
    rows = pos.shape[0]
    groups = LANES // (QK_ROPE // 2)
    q4 = rows // groups
    width = LANES // groups
    lane_grp = lax.broadcasted_iota(jnp.int32, (q4, LANES), 1) // width

    def by_group(vals):
        out = vals[groups - 1]
        for g in range(groups - 2, -1, -1):
            out = jnp.where(lane_grp == g, vals[g], out)
        return out

    ang = by_group([pos[g * q4:(g + 1) * q4, :] for g in range(groups)]) * invf
    tables = []
    for dense in (jnp.cos(ang), jnp.sin(ang)):
        rolled = [dense] + [pltpu.roll(dense, k * width, axis=1) for k in range(1, groups)]
        tables.append(jnp.concatenate(
            [by_group([rolled[(j - g) % groups] for j in range(groups)]) for g in range(groups)],
            axis=0))
    return tables[0], tables[1] * sign


def _lru_rates(lam):
    nlam = -lam
    softplus = jnp.maximum(nlam, 0.0) + jnp.log1p(jnp.exp(-jnp.abs(nlam)))
    return (-0.5 * LRU_C * math.log2(math.e)) * softplus


def _lru_gate_block(xa, n, rate, wg_ref, br_ref, bi_ref):
    blk = xa.shape[1]
    c0, c1 = n * blk, (n + 1) * blk
    gates = _dot(xa.astype(BF16), wg_ref[n])
    tr = jnp.tanh(gates[:, 0:blk] + br_ref[:, c0:c1])
    ti = jnp.tanh(gates[:, blk:2 * blk] + bi_ref[:, c0:c1])
    a = jnp.exp2(tr * rate[:, c0:c1] + rate[:, c0:c1])
    gap = 1.0 - a * a
    root = jnp.where(gap > 0.0, gap * lax.rsqrt(gap), 0.0)
    xh = 0.5 * xa
    return a, root * (ti * xh + xh)


def _lru_scan(a_s, u_s, h_s, out_ref, steps, reverse):
    def step(j, h):
        t = (steps - 1 - j) if reverse else j
        r0 = pl.multiple_of(t * SUBLANES, SUBLANES)
        h = a_s[:, pl.ds(r0, SUBLANES), :] * h + u_s[:, pl.ds(r0, SUBLANES), :]
        out_ref[:, pl.ds(r0, SUBLANES), :] = h
        return h

    h_s[...] = lax.fori_loop(0, steps, step, h_s[...], unroll=8)


def _inproj_kernel(h_ref, hp_ref, hn_ref, pos_ref, g_ref, w_ref, qg_ref, wqm_ref, kvg_ref,
                   wk_ref, wv_ref, invf_ref, sign_ref, cw_ref, cb_ref, wgl_ref, br_ref, bi_ref,
                   lam_ref,
                   zg_ref, ga_ref, gb_ref, q_ref, k_ref, v_ref, hf_ref, xa_ref,
                   xpad, a_s, u_s, h_s, hf_s, *, cols, q_scale):
    i = pl.program_id(0)
    nb, steps, d = h_ref.shape
    rows = nb * steps
    halo = hp_ref.shape[1]
    un = _rmsnorm(h_ref[...].reshape(rows, d), g_ref[...]).astype(BF16)

    def proj(name):
        c0, c1 = cols[name]
        return _dot(un, w_ref[:, c0:c1])

    def per_batch(val):
        return val.reshape(nb, steps, val.shape[-1])

    @pl.when(i == 0)
    def _():
        h_s[...] = jnp.zeros_like(h_s)

    zx = proj("zx")
    h_halo = jnp.concatenate([hp_ref[...].reshape(nb * halo, d), hn_ref[...].reshape(nb * halo, d)])
    c0, c1 = cols["zx"]
    zx_halo = _dot(_rmsnorm(h_halo, g_ref[...]).astype(BF16), w_ref[:, c0:c1])
    zx_prev = jnp.where(i > 0, zx_halo[0:nb * halo], 0.0)
    zx_next = jnp.where(i < pl.num_programs(0) - 1, zx_halo[nb * halo:], 0.0)
    first = halo * nb
    for n in range(LRU_BLOCKS):
        lanes = slice(n * LANES, (n + 1) * LANES)
        for b in range(nb):
            xpad[n, pl.ds(b, halo, stride=nb), :] = zx_prev[b * halo:(b + 1) * halo, lanes]
            xpad[n, pl.ds(first + b, steps, stride=nb), :] = zx[b * steps:(b + 1) * steps, lanes]
            xpad[n, pl.ds(first + rows + b, halo, stride=nb), :] = (
                zx_next[b * halo:(b + 1) * halo, lanes])

    rate = _lru_rates(lam_ref[...])
    for n in range(LRU_BLOCKS):
        lanes = slice(n * LANES, (n + 1) * LANES)
        xa = cb_ref[:, lanes]
        for k in range(CONV_WIDTH):
            r0 = first + (k - CONV_WIDTH // 2) * SUBLANES
            xa = xa + xpad[n, r0:r0 + rows, :] * cw_ref[k:k + 1, lanes]
        xa_ref[n] = xa.astype(xa_ref.dtype)
        a_s[n], u_s[n] = _lru_gate_block(xa, n, rate, wgl_ref, br_ref, bi_ref)

    zg_ref[...] = per_batch(proj("zg").astype(BF16))
    ga_ref[...] = per_batch(proj("ga").astype(BF16))
    gb_ref[...] = per_batch(proj("gb").astype(BF16))

    cosv, sinv = _rope_tables(pos_ref[...].reshape(rows, 1), invf_ref[...], sign_ref[...])
    lane = lax.broadcasted_iota(jnp.int32, (rows, LANES), 1)
    low = lane < QK_ROPE
    table = jnp.where(low, cosv, sinv)

    def rotate(x, tab):
        y = x * tab
        return jnp.where(low, y + pltpu.roll(y, QK_ROPE, axis=1), 0.0)

    cqn = _rmsnorm(proj("cq"), qg_ref[...]).astype(BF16)
    qm = _dot(cqn, wqm_ref[...])
    q_table = table * q_scale
    for h in range(MLA_HEADS):
        m0 = h * QK_PAD
        q_ref[:, h, :, 0:QK_NOPE] = per_batch((qm[:, m0:m0 + QK_NOPE] * q_scale).astype(BF16))
        rope = rotate(qm[:, m0 + QK_NOPE:m0 + QK_PAD], q_table)
        q_ref[:, h, :, QK_NOPE:QK_PAD] = per_batch(rope.astype(BF16))

    ckvn = _rmsnorm(proj("ckv"), kvg_ref[...]).astype(BF16)
    kn = _dot(ckvn, wk_ref[...])
    vv = _dot(ckvn, wv_ref[...])
    kpe = per_batch(jnp.where(lane == STAB_LANE - QK_NOPE, 1.0,
                              rotate(proj("kr"), table)).astype(BF16))
    for h in range(MLA_HEADS):
        k_ref[:, h, :, 0:QK_NOPE] = per_batch(kn[:, h * QK_NOPE:(h + 1) * QK_NOPE].astype(BF16))
        k_ref[:, h, :, QK_NOPE:QK_PAD] = kpe
        v_ref[:, h, :, :] = per_batch(vv[:, h * V_HEAD:(h + 1) * V_HEAD].astype(BF16))

    _lru_scan(a_s, u_s, h_s, hf_s, steps, reverse=False)
    hf_ref[...] = hf_s[...].astype(hf_ref.dtype)


def _inproj(h1, posf, g, w, cols, qg, wqm, kvg, wk, wv, invf, sign, q_scale, lru_weights):
    b, s, d = h1.shape
    t = STEPS
    halo = SUBLANES
    width = cols["zx"][1] - cols["zx"][0]
    nblk = width // LANES
    rows = t * b
    bm = lambda x: pl.BlockSpec((b, t, x), lambda i: (0, i, 0))
    heads = lambda x: pl.BlockSpec((b, MLA_HEADS, t, x), lambda i: (0, 0, i, 0))
    tm = pl.BlockSpec((nblk, rows, LANES), lambda i: (0, i, 0))
    per_halo = t // halo
    prev = pl.BlockSpec((b, halo, d), lambda i: (0, jnp.maximum(i * per_halo - 1, 0), 0))
    nxt = pl.BlockSpec((b, halo, d), lambda i: (0, jnp.minimum((i + 1) * per_halo,
                                                               s // halo - 1), 0))
    weights = (g, w, qg, wqm, kvg, wk, wv, invf, sign, *lru_weights)
    kernel = functools.partial(_inproj_kernel, cols=cols, q_scale=q_scale)
    return pl.pallas_call(
        kernel,
        out_shape=[
            jax.ShapeDtypeStruct((b, s, width), BF16),
            jax.ShapeDtypeStruct((b, s, d), BF16),
            jax.ShapeDtypeStruct((b, s, d), BF16),
            jax.ShapeDtypeStruct((b, MLA_HEADS, s, QK_PAD), BF16),
            jax.ShapeDtypeStruct((b, MLA_HEADS, s, QK_PAD), BF16),
            jax.ShapeDtypeStruct((b, MLA_HEADS, s, V_HEAD), BF16),
            jax.ShapeDtypeStruct((nblk, s * b, LANES), BF16),
            jax.ShapeDtypeStruct((nblk, s * b, LANES), BF16),
        ],
        grid=(s // t,),
        in_specs=[bm(d), prev, nxt, bm(1)] + [_resident(a.shape) for a in weights],
        out_specs=[bm(width), bm(d), bm(d), heads(QK_PAD), heads(QK_PAD), heads(V_HEAD), tm, tm],
        scratch_shapes=[
            pltpu.VMEM((nblk, rows + 2 * halo * b, LANES), F32),
            pltpu.VMEM((nblk, rows, LANES), F32),
            pltpu.VMEM((nblk, rows, LANES), F32),
            pltpu.VMEM((nblk, SUBLANES, LANES), F32),
            pltpu.VMEM((nblk, rows, LANES), F32),
        ],
        compiler_params=_params(("arbitrary",)),
        name="inproj",
    )(h1, h1, h1, posf, *weights)


def _lru_bwd_kernel(xa_ref, hf_ref, wg_ref, br_ref, bi_ref, lam_ref, o_ref, a_s, u_s, h_s, hb_s,
                    *, steps):
    @pl.when(pl.program_id(0) == 0)
    def _():
        h_s[...] = jnp.zeros_like(h_s)

    rate = _lru_rates(lam_ref[...])
    for n in range(LRU_BLOCKS):
        a_s[n], u_s[n] = _lru_gate_block(xa_ref[n].astype(F32), n, rate, wg_ref, br_ref, bi_ref)
    _lru_scan(a_s, u_s, h_s, hb_s, steps, reverse=True)

    hb_s[...] = hb_s[...] + hf_ref[...].astype(F32)
    blk = xa_ref.shape[2]
    for n in range(LRU_BLOCKS):
        for b in range(SUBLANES):
            o_ref[b, :, n * blk:(n + 1) * blk] = (
                hb_s[n, pl.ds(b, steps, stride=SUBLANES), :].astype(o_ref.dtype))


def _lru_bwd(xa3, hf3, wg, br, bi, lam):
    nblk, n_rows, blk = xa3.shape
    steps = STEPS
    rows = steps * SUBLANES
    n_tiles = n_rows // rows
    cur = pl.BlockSpec((nblk, rows, blk), lambda i: (0, n_tiles - 1 - i, 0))
    weights = (wg, br, bi, lam)
    return pl.pallas_call(
        functools.partial(_lru_bwd_kernel, steps=steps),
        out_shape=jax.ShapeDtypeStruct((SUBLANES, n_rows // SUBLANES, nblk * blk), BF16),
        grid=(n_tiles,),
        in_specs=[cur, cur] + [_resident(a.shape) for a in weights],
        out_specs=pl.BlockSpec((SUBLANES, steps, nblk * blk), lambda i: (0, n_tiles - 1 - i, 0)),
        scratch_shapes=[
            pltpu.VMEM((nblk, rows, blk), F32),
            pltpu.VMEM((nblk, rows, blk), F32),
            pltpu.VMEM((nblk, SUBLANES, blk), F32),
            pltpu.VMEM((nblk, rows, blk), F32),
        ],
        compiler_params=_params(("arbitrary",)),
        name="lru_bwd",
    )(xa3, hf3, *weights)


def _attn_kernel(q_ref, k_ref, v_ref, o_ref, p_s, vt_s, kmax_s):
    dv = v_ref.shape[-1]
    n_blocks, _, tb = p_s.shape
    lane = lax.broadcasted_iota(jnp.int32, (1, q_ref.shape[-1]), 1)

    @pl.when(pl.program_id(2) == 0)
    def _():
        vt_s[0:dv, :] = v_ref[0, 0].astype(F32).T.astype(BF16)
        row = lax.broadcasted_iota(jnp.int32, (BF16_ROWS, vt_s.shape[1]), 0)
        vt_s[dv:dv + BF16_ROWS, :] = jnp.where(row == 0, 1.0, 0.0).astype(BF16)
        kf = k_ref[0, 0].astype(F32)
        ksq = jnp.max(jnp.sum(kf * kf, axis=1, keepdims=True), axis=0, keepdims=True)
        kmax_s[...] = jnp.broadcast_to(jnp.sqrt(ksq), kmax_s.shape)

    def block(i, exact):
        r0 = i * tb
        qf = q_ref[0, 0, r0:r0 + tb, :].astype(F32)
        bound = jnp.sqrt(jnp.sum(qf * qf, axis=1, keepdims=True)) * kmax_s[...]
        qs = jnp.where(lane == STAB_LANE, -bound, qf).astype(BF16)
        st = _dot_nt(k_ref[0, 0], qs)
        if exact:
            st = st - jnp.max(st, axis=0, keepdims=True)
        p_s[i] = jnp.exp2(st).astype(BF16)
        ot = _dot(vt_s[...], p_s[i])
        denom = ot[dv:dv + 1, :]
        o_ref[0, r0:r0 + tb, :] = (ot[0:dv, :] / denom).T.astype(o_ref.dtype)
        return jnp.min(denom)

    smallest = block(0, exact=False)
    for i in range(1, n_blocks):
        smallest = jnp.minimum(smallest, block(i, exact=False))

    @pl.when(jnp.logical_not(smallest >= DENOM_FLOOR))
    def _():
        for i in range(n_blocks):
            block(i, exact=True)


def _attn(q, k, v):
    b, nh, s, dq = q.shape
    dv = v.shape[-1]
    n_blocks = 2
    tq = n_blocks * Q_BLOCK
    return pl.pallas_call(
        _attn_kernel,
        out_shape=jax.ShapeDtypeStruct((b, s, nh * dv), BF16),
        grid=(b, nh, s // tq),
        in_specs=[
            pl.BlockSpec((1, 1, tq, dq), lambda i, h, j: (i, h, j, 0)),
            pl.BlockSpec((1, 1, s, dq), lambda i, h, j: (i, h, 0, 0)),
            pl.BlockSpec((1, 1, s, dv), lambda i, h, j: (i, h, 0, 0)),
        ],
        out_specs=pl.BlockSpec((1, tq, dv), lambda i, h, j: (i, j, h)),
        scratch_shapes=[
            pltpu.VMEM((n_blocks, s, Q_BLOCK), BF16),
            pltpu.VMEM((dv + BF16_ROWS, s), BF16),
            pltpu.VMEM((1, dq), F32),
        ],
        compiler_params=_params(("parallel", "parallel", "arbitrary")),
        name="attn",
    )(q, k, v)


def _tail_kernel(h_ref, ys_ref, zg_ref, ob_ref, ga_ref, gb_ref, p_ref, wl_ref, wm_ref, wo_ref,
                 g_ref, wg_ref, wu_ref, wd_ref, pg_ref, wpg_ref, wpp_ref, ppg_ref, fg_ref, o_ref):
    ya_in = ys_ref[0].astype(F32) * jax.nn.gelu(zg_ref[0].astype(F32))
    ya = _dot(ya_in.astype(BF16), wl_ref[...])
    yb = _dot(ob_ref[0], wm_ref[...])
    merged = (jax.nn.sigmoid(ga_ref[0].astype(F32)) * ya
              + jax.nn.sigmoid(gb_ref[0].astype(F32)) * yb)
    h = h_ref[0] + _dot(merged.astype(BF16), wo_ref[...])

    h = _swiglu_half(h, g_ref[...], wg_ref, wu_ref, wd_ref)
    gate = jax.nn.sigmoid(_dot(_rmsnorm(h, pg_ref[...]).astype(BF16), wpg_ref[...]))
    emb = _rmsnorm(_dot(p_ref[0].astype(BF16), wpp_ref[...]), ppg_ref[...])
    h = h + gate * emb
    o_ref[0] = _rmsnorm(h, fg_ref[...])


def _tail(h1, ysum, zg, ob, ga, gb, p, weights):
    b, s, d = h1.shape
    t = FFN_ROWS
    row = pl.BlockSpec((1, t, d), lambda i, j: (i, j, 0))
    prow = pl.BlockSpec((1, t, p.shape[-1]), lambda i, j: (i, j, 0))
    return pl.pallas_call(
        _tail_kernel,
        out_shape=jax.ShapeDtypeStruct((b, s, d), F32),
        grid=(b, s // t),
        in_specs=[row] * 6 + [prow] + [_resident(a.shape) for a in weights],
        out_specs=row,
        compiler_params=_params(("parallel", "parallel")),
        name="tail",
    )(h1, ysum, zg, ob, ga, gb, p, *weights)


def _swap_halves(w):
    half = w.shape[-1] // 2
    return jnp.concatenate([w[..., half:], w[..., :half]], axis=-1)


def _in_weights(w_in, d, lru_w, q_lora, kv_lora):
    sizes = (lru_w, lru_w, q_lora, kv_lora, QK_ROPE, d, d)
    offs = [0]
    for sz in sizes:
        offs.append(offs[-1] + sz)
    zx, zg, cq, ckv, kr, ga, gb = (w_in[:, offs[i]:offs[i + 1]] for i in range(7))
    kr2 = jnp.concatenate([kr, _swap_halves(kr)], axis=1)
    parts = {"zx": zx, "zg": zg, "cq": cq, "ckv": ckv, "kr": kr2, "ga": ga, "gb": gb}
    cols, c0 = {}, 0
    for name, part in parts.items():
        cols[name] = (c0, c0 + part.shape[1])
        c0 += part.shape[1]
    return jnp.concatenate(list(parts.values()), axis=1).astype(BF16), cols


def _q_weights(w_uq):
    r = w_uq.shape[0]
    w = w_uq.reshape(r, MLA_HEADS, QK_NOPE + QK_ROPE)
    full = jnp.concatenate([w, _swap_halves(w[:, :, QK_NOPE:])], axis=-1)
    assert full.shape[-1] == QK_PAD
    return full.reshape(r, MLA_HEADS * QK_PAD).astype(BF16)


def kernel(x, p, positions, ffn1_norm, ffn1_w_gate, ffn1_w_up, ffn1_w_down, mix_norm, w_in, conv_w, conv_b, lru_w_r, lru_b_r, lru_w_i, lru_b_i, lru_lambda, w_lru_out, q_norm, w_uq, kv_norm, w_ukv, w_mla_out, w_o, ffn2_norm, ffn2_w_gate, ffn2_w_up, ffn2_w_down, ple_norm, ple_w_gate, ple_w_proj, ple_proj_norm, final_norm):
    b, s, d = x.shape
    depth = ffn1_norm.shape[0]
    lru_w = conv_w.shape[-1]
    q_lora, kv_lora = q_norm.shape[-1], kv_norm.shape[-1]
    assert b == SUBLANES and lru_w == d
    assert depth == 1
    assert s % FFN_ROWS == 0 and s % STEPS == 0
    assert s % (2 * Q_BLOCK) == 0

    row = lambda v: v.reshape(1, -1).astype(F32)
    posf = positions.astype(F32).reshape(b, s, 1)
    half = QK_ROPE // 2
    inv_freq = ROPE_THETA ** (-jnp.arange(0, QK_ROPE, 2, dtype=F32) / QK_ROPE)
    invf = jnp.tile(inv_freq, LANES // half).reshape(1, LANES)
    sign = jnp.tile(jnp.concatenate([-jnp.ones(half, F32), jnp.ones(half, F32)]),
                    LANES // QK_ROPE).reshape(1, LANES)
    q_scale = (QK_NOPE + QK_ROPE) ** -0.5 * math.log2(math.e)

    i = 0
    h1 = _ffn(x, row(ffn1_norm[i]), ffn1_w_gate[i].astype(BF16), ffn1_w_up[i].astype(BF16),
              ffn1_w_down[i].astype(BF16))

    w_all, cols = _in_weights(w_in[i], d, lru_w, q_lora, kv_lora)
    wqm = _q_weights(w_uq[i])
    w_kv = w_ukv[i].reshape(kv_lora, MLA_HEADS, QK_NOPE + V_HEAD)
    wk = w_kv[:, :, :QK_NOPE].reshape(kv_lora, MLA_HEADS * QK_NOPE).astype(BF16)
    wv = w_kv[:, :, QK_NOPE:].reshape(kv_lora, MLA_HEADS * V_HEAD).astype(BF16)
    lru_args = lambda dr: (
        (0.5 * jnp.concatenate([lru_w_r[i, dr], lru_w_i[i, dr]], axis=-1)).astype(BF16),
        row(0.5 * lru_b_r[i, dr]), row(0.5 * lru_b_i[i, dr]), row(lru_lambda[i, dr]))
    conv = (conv_w[i].astype(F32), row(conv_b[i]))
    zg, ga, gb, q, k, v, hf3, xa3 = _inproj(
        h1, posf, row(mix_norm[i]), w_all, cols, row(q_norm[i]), wqm, row(kv_norm[i]),
        wk, wv, invf, sign, q_scale, (*conv, *lru_args(0)))
    ysum = _lru_bwd(xa3, hf3, *lru_args(1))

    ob = _attn(q, k, v)

    tail_weights = (
        w_lru_out[i].astype(BF16), w_mla_out[i].astype(BF16), w_o[i].astype(BF16),
        row(ffn2_norm[i]), ffn2_w_gate[i].astype(BF16), ffn2_w_up[i].astype(BF16),
        ffn2_w_down[i].astype(BF16), row(ple_norm[i]), ple_w_gate[i].astype(BF16),
        ple_w_proj[i].astype(BF16), row(ple_proj_norm[i]), row(final_norm))
    return _tail(h1, ysum, zg, ob, ga, gb, p.reshape(b, s, -1), tail_weights)
```

```python
import functools
import math

import jax
import jax.numpy as jnp
from jax import lax
from jax.experimental import pallas as pl
from jax.experimental.pallas import tpu as pltpu

F32 = jnp.float32
BF16 = jnp.bfloat16

LRU_BLOCKS = 8
CONV_WIDTH = 4
LRU_C = 8.0
MLA_HEADS = 8
QK_NOPE = 128
QK_ROPE = 64
V_HEAD = 128
ROPE_THETA = 10000.0
EPS = 1e-6

LANES = 128
SUBLANES = 8
BF16_ROWS = 16
QK_PAD = 256
STAB_LANE = QK_NOPE + QK_ROPE
DENOM_FLOOR = 2.0 ** -64
VMEM_LIMIT_BYTES = 56 * 1024 * 1024

FFN_ROWS = 512
STEPS = 64
Q_BLOCK = 512
FF_CHUNK = 512


def _resident(shape):
    nd = len(shape)
    return pl.BlockSpec(shape, lambda *_: (0,) * nd, pipeline_mode=pl.Buffered(1))


def _params(semantics):
    return pltpu.CompilerParams(dimension_semantics=semantics,
                                vmem_limit_bytes=VMEM_LIMIT_BYTES)


def _rmsnorm(x, g):
    ms = jnp.mean(x * x, axis=-1, keepdims=True)
    return x * lax.rsqrt(ms + EPS) * g


def _dot(a, b):
    return jnp.dot(a, b, preferred_element_type=F32)


def _dot_nt(a, b):
    return lax.dot_general(a, b, (((1,), (1,)), ((), ())), preferred_element_type=F32)


def _swiglu_half(x, g, wg_ref, wu_ref, wd_ref):
    d_ff = wg_ref.shape[1]
    xn = _rmsnorm(x, g).astype(BF16)
    y = None
    for c0 in range(0, d_ff, FF_CHUNK):
        c1 = min(c0 + FF_CHUNK, d_ff)
        gate = _dot(xn, wg_ref[:, c0:c1])
        up = _dot(xn, wu_ref[:, c0:c1])
        act = (gate * jax.nn.sigmoid(gate) * up).astype(BF16)
        part = _dot(act, wd_ref[c0:c1, :])
        y = part if y is None else y + part
    return x + 0.5 * y


def _ffn_kernel(x_ref, g_ref, wg_ref, wu_ref, wd_ref, o_ref):
    o_ref[0] = _swiglu_half(x_ref[0], g_ref[...], wg_ref, wu_ref, wd_ref)


def _ffn(x, g, wg, wu, wd):
    b, s, d = x.shape
    t = FFN_ROWS
    row = pl.BlockSpec((1, t, d), lambda i, j: (i, j, 0))
    return pl.pallas_call(
        _ffn_kernel,
        out_shape=jax.ShapeDtypeStruct((b, s, d), F32),
        grid=(b, s // t),
        in_specs=[row, _resident(g.shape), _resident(wg.shape), _resident(wu.shape),
                  _resident(wd.shape)],
        out_specs=row,
        compiler_params=_params(("parallel", "parallel")),
        name="ffn",
    )(x, g, wg, wu, wd)


def _rope_tables(pos, invf, sign):
    rows = pos.shape[0]
    groups = LANES // (QK_ROPE // 2)
    q4 = rows // groups
    width = LANES // groups
    lane_grp = lax.broadcasted_iota(jnp.int32, (q4, LANES), 1) // width

    def by_group(vals):
        out = vals[groups - 1]
        for g in range(groups - 2, -1, -1):
            out = jnp.where(lane_grp == g, vals[g], out)
        return out

    ang = by_group([pos[g * q4:(g + 1) * q4, :] for g in range(groups)]) * invf
    tables = []
    for dense in (jnp.cos(ang), jnp.sin(ang)):
        rolled = [dense] + [pltpu.roll(dense, k * width, axis=1) for k in range(1, groups)]
        tables.append(jnp.concatenate(
            [by_group([rolled[(j - g) % groups] for j in range(groups)]) for g in range(groups)],
            axis=0))
    return tables[0], tables[1] * sign


def _lru_rates(lam):
    nlam = -lam
    softplus = jnp.maximum(nlam, 0.0) + jnp.log1p(jnp.exp(-jnp.abs(nlam)))
    return (-0.5 * LRU_C * math.log2(math.e)) * softplus


def _lru_gate_block(xa, n, rate, wg_ref, br_ref, bi_ref):
    blk = xa.shape[1]
    c0, c1 = n * blk, (n + 1) * blk
    gates = _dot(xa.astype(BF16), wg_ref[n])
    tr = jnp.tanh(gates[:, 0:blk] + br_ref[:, c0:c1])
    ti = jnp.tanh(gates[:, blk:2 * blk] + bi_ref[:, c0:c1])
    a = jnp.exp2(tr * rate[:, c0:c1] + rate[:, c0:c1])
    gap = 1.0 - a * a
    root = jnp.where(gap > 0.0, gap * lax.rsqrt(gap), 0.0)
    xh = 0.5 * xa
    return a, root * (ti * xh + xh)


def _lru_scan(a_s, u_s, h_s, out_ref, steps, reverse):
    def step(j, h):
        t = (steps - 1 - j) if reverse else j
        r0 = pl.multiple_of(t * SUBLANES, SUBLANES)
        h = a_s[:, pl.ds(r0, SUBLANES), :] * h + u_s[:, pl.ds(r0, SUBLANES), :]
        out_ref[:, pl.ds(r0, SUBLANES), :] = h
        return h

    h_s[...] = lax.fori_loop(0, steps, step, h_s[...], unroll=8)


def _inproj_kernel(h_ref, hp_ref, hn_ref, pos_ref, g_ref, w_ref, qg_ref, wqm_ref, kvg_ref,
                   wk_ref, wv_ref, invf_ref, sign_ref, cw_ref, cb_ref, wgl_ref, br_ref, bi_ref,
                   lam_ref,
                   zg_ref, ga_ref, gb_ref, q_ref, k_ref, v_ref, hf_ref, xa_ref,
                   xpad, a_s, u_s, h_s, hf_s, *, cols, q_scale):
    i = pl.program_id(0)
    nb, steps, d = h_ref.shape
    rows = nb * steps
    halo = hp_ref.shape[1]
    un = _rmsnorm(h_ref[...].reshape(rows, d), g_ref[...]).astype(BF16)

    def proj(name):
        c0, c1 = cols[name]
        return _dot(un, w_ref[:, c0:c1])

    def per_batch(val):
        return val.reshape(nb, steps, val.shape[-1])

    @pl.when(i == 0)
    def _():
        h_s[...] = jnp.zeros_like(h_s)

    zx = proj("zx")
    h_halo = jnp.concatenate([hp_ref[...].reshape(nb * halo, d), hn_ref[...].reshape(nb * halo, d)])
    c0, c1 = cols["zx"]
    zx_halo = _dot(_rmsnorm(h_halo, g_ref[...]).astype(BF16), w_ref[:, c0:c1])
    zx_prev = jnp.where(i > 0, zx_halo[0:nb * halo], 0.0)
    zx_next = jnp.where(i < pl.num_programs(0) - 1, zx_halo[nb * halo:], 0.0)
    first = halo * nb
    for n in range(LRU_BLOCKS):
        lanes = slice(n * LANES, (n + 1) * LANES)
        for b in range(nb):
            xpad[n, pl.ds(b, halo, stride=nb), :] = zx_prev[b * halo:(b + 1) * halo, lanes]
            xpad[n, pl.ds(first + b, steps, stride=nb), :] = zx[b * steps:(b + 1) * steps, lanes]
            xpad[n, pl.ds(first + rows + b, halo, stride=nb), :] = (
                zx_next[b * halo:(b + 1) * halo, lanes])

    rate = _lru_rates(lam_ref[...])
    for n in range(LRU_BLOCKS):
        lanes = slice(n * LANES, (n + 1) * LANES)
        xa = cb_ref[:, lanes]
        for k in range(CONV_WIDTH):
            r0 = first + (k - CONV_WIDTH // 2) * SUBLANES
            xa = xa + xpad[n, r0:r0 + rows, :] * cw_ref[k:k + 1, lanes]
        xa_ref[n] = xa.astype(xa_ref.dtype)
        a_s[n], u_s[n] = _lru_gate_block(xa, n, rate, wgl_ref, br_ref, bi_ref)

    zg_ref[...] = per_batch(proj("zg").astype(BF16))
    ga_ref[...] = per_batch(proj("ga").astype(BF16))
    gb_ref[...] = per_batch(proj("gb").astype(BF16))

    cosv, sinv = _rope_tables(pos_ref[...].reshape(rows, 1), invf_ref[...], sign_ref[...])
    lane = lax.broadcasted_iota(jnp.int32, (rows, LANES), 1)
    low = lane < QK_ROPE
    table = jnp.where(low, cosv, sinv)

    def rotate(x, tab):
        y = x * tab
        return jnp.where(low, y + pltpu.roll(y, QK_ROPE, axis=1), 0.0)

    cqn = _rmsnorm(proj("cq"), qg_ref[...]).astype(BF16)
    qm = _dot(cqn, wqm_ref[...])
    q_table = table * q_scale
    for h in range(MLA_HEADS):
        m0 = h * QK_PAD
        q_ref[:, h, :, 0:QK_NOPE] = per_batch((qm[:, m0:m0 + QK_NOPE] * q_scale).astype(BF16))
        rope = rotate(qm[:, m0 + QK_NOPE:m0 + QK_PAD], q_table)
        q_ref[:, h, :, QK_NOPE:QK_PAD] = per_batch(rope.astype(BF16))

    ckvn = _rmsnorm(proj("ckv"), kvg_ref[...]).astype(BF16)
    kn = _dot(ckvn, wk_ref[...])
    vv = _dot(ckvn, wv_ref[...])
    kpe = per_batch(jnp.where(lane == STAB_LANE - QK_NOPE, 1.0,
                              rotate(proj("kr"), table)).astype(BF16))
    for h in range(MLA_HEADS):
        k_ref[:, h, :, 0:QK_NOPE] = per_batch(kn[:, h * QK_NOPE:(h + 1) * QK_NOPE].astype(BF16))
        k_ref[:, h, :, QK_NOPE:QK_PAD] = kpe
        v_ref[:, h, :, :] = per_batch(vv[:, h * V_HEAD:(h + 1) * V_HEAD].astype(BF16))

    _lru_scan(a_s, u_s, h_s, hf_s, steps, reverse=False)
    hf_ref[...] = hf_s[...].astype(hf_ref.dtype)


def _inproj(h1, posf, g, w, cols, qg, wqm, kvg, wk, wv, invf, sign, q_scale, lru_weights):
    b, s, d = h1.shape
    t = STEPS
    halo = SUBLANES
    width = cols["zx"][1] - cols["zx"][0]
    nblk = width // LANES
    rows = t * b
    bm = lambda x: pl.BlockSpec((b, t, x), lambda i: (0, i, 0))
    heads = lambda x: pl.BlockSpec((b, MLA_HEADS, t, x), lambda i: (0, 0, i, 0))
    tm = pl.BlockSpec((nblk, rows, LANES), lambda i: (0, i, 0))
    per_halo = t // halo
    prev = pl.BlockSpec((b, halo, d), lambda i: (0, jnp.maximum(i * per_halo - 1, 0), 0))
    nxt = pl.BlockSpec((b, halo, d), lambda i: (0, jnp.minimum((i + 1) * per_halo,
                                                               s // halo - 1), 0))
    weights = (g, w, qg, wqm, kvg, wk, wv, invf, sign, *lru_weights)
    kernel = functools.partial(_inproj_kernel, cols=cols, q_scale=q_scale)
    return pl.pallas_call(
        kernel,
        out_shape=[
            jax.ShapeDtypeStruct((b, s, width), BF16),
            jax.ShapeDtypeStruct((b, s, d), BF16),
            jax.ShapeDtypeStruct((b, s, d), BF16),
            jax.ShapeDtypeStruct((b, MLA_HEADS, s, QK_PAD), BF16),
            jax.ShapeDtypeStruct((b, MLA_HEADS, s, QK_PAD), BF16),
            jax.ShapeDtypeStruct((b, MLA_HEADS, s, V_HEAD), BF16),
            jax.ShapeDtypeStruct((nblk, s * b, LANES), BF16),
            jax.ShapeDtypeStruct((nblk, s * b, LANES), BF16),
        ],
        grid=(s // t,),
        in_specs=[bm(d), prev, nxt, bm(1)] + [_resident(a.shape) for a in weights],
        out_specs=[bm(width), bm(d), bm(d), heads(QK_PAD), heads(QK_PAD), heads(V_HEAD), tm, tm],
        scratch_shapes=[
            pltpu.VMEM((nblk, rows + 2 * halo * b, LANES), F32),
            pltpu.VMEM((nblk, rows, LANES), F32),
            pltpu.VMEM((nblk, rows, LANES), F32),
            pltpu.VMEM((nblk, SUBLANES, LANES), F32),
            pltpu.VMEM((nblk, rows, LANES), F32),
        ],
        compiler_params=_params(("arbitrary",)),
        name="inproj",
    )(h1, h1, h1, posf, *weights)


def _lru_bwd_kernel(xa_ref, hf_ref, wg_ref, br_ref, bi_ref, lam_ref, o_ref, a_s, u_s, h_s, hb_s,
                    *, steps):
    @pl.when(pl.program_id(0) == 0)
    def _():
        h_s[...] = jnp.zeros_like(h_s)

    rate = _lru_rates(lam_ref[...])
    for n in range(LRU_BLOCKS):
        a_s[n], u_s[n] = _lru_gate_block(xa_ref[n].astype(F32), n, rate, wg_ref, br_ref, bi_ref)
    _lru_scan(a_s, u_s, h_s, hb_s, steps, reverse=True)

    hb_s[...] = hb_s[...] + hf_ref[...].astype(F32)
    blk = xa_ref.shape[2]
    for n in range(LRU_BLOCKS):
        for b in range(SUBLANES):
            o_ref[b, :, n * blk:(n + 1) * blk] = (
                hb_s[n, pl.ds(b, steps, stride=SUBLANES), :].astype(o_ref.dtype))


def _lru_bwd(xa3, hf3, wg, br, bi, lam):
    nblk, n_rows, blk = xa3.shape
    steps = STEPS
    rows = steps * SUBLANES
    n_tiles = n_rows // rows
    cur = pl.BlockSpec((nblk, rows, blk), lambda i: (0, n_tiles - 1 - i, 0))
    weights = (wg, br, bi, lam)
    return pl.pallas_call(
        functools.partial(_lru_bwd_kernel, steps=steps),
        out_shape=jax.ShapeDtypeStruct((SUBLANES, n_rows // SUBLANES, nblk * blk), BF16),
        grid=(n_tiles,),
        in_specs=[cur, cur] + [_resident(a.shape) for a in weights],
        out_specs=pl.BlockSpec((SUBLANES, steps, nblk * blk), lambda i: (0, n_tiles - 1 - i, 0)),
        scratch_shapes=[
            pltpu.VMEM((nblk, rows, blk), F32),
            pltpu.VMEM((nblk, rows, blk), F32),
            pltpu.VMEM((nblk, SUBLANES, blk), F32),
            pltpu.VMEM((nblk, rows, blk), F32),
        ],
        compiler_params=_params(("arbitrary",)),
        name="lru_bwd",
    )(xa3, hf3, *weights)


def _attn_kernel(q_ref, k_ref, v_ref, o_ref, p_s, vt_s):
    dv = v_ref.shape[-1]
    tb = p_s.shape[-1]
    n_blocks = q_ref.shape[2] // tb
    lane = lax.broadcasted_iota(jnp.int32, (1, q_ref.shape[-1]), 1)

    vt_s[0:dv, :] = v_ref[0, 0].astype(F32).T.astype(BF16)
    row = lax.broadcasted_iota(jnp.int32, (BF16_ROWS, vt_s.shape[1]), 0)
    vt_s[dv:dv + BF16_ROWS, :] = jnp.where(row == 0, 1.0, 0.0).astype(BF16)
    kf = k_ref[0, 0].astype(F32)
    kmax = jnp.sqrt(jnp.max(jnp.sum(kf * kf, axis=1, keepdims=True), axis=0, keepdims=True))

    def block(i, exact):
        r0 = i * tb
        slot = i % p_s.shape[0]
        qf = q_ref[0, 0, r0:r0 + tb, :].astype(F32)
        bound = jnp.sqrt(jnp.sum(qf * qf, axis=1, keepdims=True)) * kmax
        qs = jnp.where(lane == STAB_LANE, -bound, qf).astype(BF16)
        st = _dot_nt(k_ref[0, 0], qs)
        if exact:
            st = st - jnp.max(st, axis=0, keepdims=True)
        p_s[slot] = jnp.exp2(st).astype(BF16)
        ot = _dot(vt_s[...], p_s[slot])
        denom = ot[dv:dv + 1, :]
        o_ref[0, r0:r0 + tb, :] = (ot[0:dv, :] / denom).T.astype(o_ref.dtype)
        return jnp.min(denom)

    smallest = block(0, exact=False)
    for i in range(1, n_blocks):
        smallest = jnp.minimum(smallest, block(i, exact=False))

    @pl.when(jnp.logical_not(smallest >= DENOM_FLOOR))
    def _():
        for i in range(n_blocks):
            block(i, exact=True)


def _attn(q, k, v):
    b, nh, s, dq = q.shape
    dv = v.shape[-1]
    head = lambda x: pl.BlockSpec((1, 1, s, x), lambda i, h: (i, h, 0, 0))
    return pl.pallas_call(
        _attn_kernel,
        out_shape=jax.ShapeDtypeStruct((b, s, nh * dv), BF16),
        grid=(b, nh),
        in_specs=[head(dq), head(dq), head(dv)],
        out_specs=pl.BlockSpec((1, s, dv), lambda i, h: (i, 0, h)),
        scratch_shapes=[
            pltpu.VMEM((2, s, Q_BLOCK), BF16),
            pltpu.VMEM((dv + BF16_ROWS, s), BF16),
        ],
        compiler_params=_params(("parallel", "parallel")),
        name="attn",
    )(q, k, v)


def _tail_kernel(h_ref, ys_ref, zg_ref, ob_ref, ga_ref, gb_ref, p_ref, wl_ref, wm_ref, wo_ref,
                 g_ref, wg_ref, wu_ref, wd_ref, pg_ref, wpg_ref, wpp_ref, ppg_ref, fg_ref, o_ref):
    ya_in = ys_ref[0].astype(F32) * jax.nn.gelu(zg_ref[0].astype(F32))
    ya = _dot(ya_in.astype(BF16), wl_ref[...])
    yb = _dot(ob_ref[0], wm_ref[...])
    merged = (jax.nn.sigmoid(ga_ref[0].astype(F32)) * ya
              + jax.nn.sigmoid(gb_ref[0].astype(F32)) * yb)
    h = h_ref[0] + _dot(merged.astype(BF16), wo_ref[...])

    h = _swiglu_half(h, g_ref[...], wg_ref, wu_ref, wd_ref)
    gate = jax.nn.sigmoid(_dot(_rmsnorm(h, pg_ref[...]).astype(BF16), wpg_ref[...]))
    emb = _rmsnorm(_dot(p_ref[0].astype(BF16), wpp_ref[...]), ppg_ref[...])
    h = h + gate * emb
    o_ref[0] = _rmsnorm(h, fg_ref[...])


def _tail(h1, ysum, zg, ob, ga, gb, p, weights):
    b, s, d = h1.shape
    t = FFN_ROWS
    row = pl.BlockSpec((1, t, d), lambda i, j: (i, j, 0))
    prow = pl.BlockSpec((1, t, p.shape[-1]), lambda i, j: (i, j, 0))
    return pl.pallas_call(
        _tail_kernel,
        out_shape=jax.ShapeDtypeStruct((b, s, d), F32),
        grid=(b, s // t),
        in_specs=[row] * 6 + [prow] + [_resident(a.shape) for a in weights],
        out_specs=row,
        compiler_params=_params(("parallel", "parallel")),
        name="tail",
    )(h1, ysum, zg, ob, ga, gb, p, *weights)


def _swap_halves(w):
    half = w.shape[-1] // 2
    return jnp.concatenate([w[..., half:], w[..., :half]], axis=-1)


def _in_weights(w_in, d, lru_w, q_lora, kv_lora):
    sizes = (lru_w, lru_w, q_lora, kv_lora, QK_ROPE, d, d)
    offs = [0]
    for sz in sizes:
        offs.append(offs[-1] + sz)
    zx, zg, cq, ckv, kr, ga, gb = (w_in[:, offs[i]:offs[i + 1]] for i in range(7))
    kr2 = jnp.concatenate([kr, _swap_halves(kr)], axis=1)
    parts = {"zx": zx, "zg": zg, "cq": cq, "ckv": ckv, "kr": kr2, "ga": ga, "gb": gb}
    cols, c0 = {}, 0
    for name, part in parts.items():
        cols[name] = (c0, c0 + part.shape[1])
        c0 += part.shape[1]
    return jnp.concatenate(list(parts.values()), axis=1).astype(BF16), cols


def _q_weights(w_uq):
    r = w_uq.shape[0]
    w = w_uq.reshape(r, MLA_HEADS, QK_NOPE + QK_ROPE)
    full = jnp.concatenate([w, _swap_halves(w[:, :, QK_NOPE:])], axis=-1)
    assert full.shape[-1] == QK_PAD
    return full.reshape(r, MLA_HEADS * QK_PAD).astype(BF16)


def kernel(x, p, positions, ffn1_norm, ffn1_w_gate, ffn1_w_up, ffn1_w_down, mix_norm, w_in, conv_w, conv_b, lru_w_r, lru_b_r, lru_w_i, lru_b_i, lru_lambda, w_lru_out, q_norm, w_uq, kv_norm, w_ukv, w_mla_out, w_o, ffn2_norm, ffn2_w_gate, ffn2_w_up, ffn2_w_down, ple_norm, ple_w_gate, ple_w_proj, ple_proj_norm, final_norm):
    b, s, d = x.shape
    depth = ffn1_norm.shape[0]
    lru_w = conv_w.shape[-1]
    q_lora, kv_lora = q_norm.shape[-1], kv_norm.shape[-1]
    assert b == SUBLANES and lru_w == d
    assert depth == 1
    assert s % FFN_ROWS == 0 and s % STEPS == 0
    assert s % Q_BLOCK == 0

    row = lambda v: v.reshape(1, -1).astype(F32)
    posf = positions.astype(F32).reshape(b, s, 1)
    half = QK_ROPE // 2
    inv_freq = ROPE_THETA ** (-jnp.arange(0, QK_ROPE, 2, dtype=F32) / QK_ROPE)
    invf = jnp.tile(inv_freq, LANES // half).reshape(1, LANES)
    sign = jnp.tile(jnp.concatenate([-jnp.ones(half, F32), jnp.ones(half, F32)]),
                    LANES // QK_ROPE).reshape(1, LANES)
    q_scale = (QK_NOPE + QK_ROPE) ** -0.5 * math.log2(math.e)

    i = 0
    h1 = _ffn(x, row(ffn1_norm[i]), ffn1_w_gate[i].astype(BF16), ffn1_w_up[i].astype(BF16),
              ffn1_w_down[i].astype(BF16))

    w_all, cols = _in_weights(w_in[i], d, lru_w, q_lora, kv_lora)
    wqm = _q_weights(w_uq[i])
    w_kv = w_ukv[i].reshape(kv_lora, MLA_HEADS, QK_NOPE + V_HEAD)
    wk = w_kv[:, :, :QK_NOPE].reshape(kv_lora, MLA_HEADS * QK_NOPE).astype(BF16)
    wv = w_kv[:, :, QK_NOPE:].reshape(kv_lora, MLA_HEADS * V_HEAD).astype(BF16)
    lru_args = lambda dr: (
        (0.5 * jnp.concatenate([lru_w_r[i, dr], lru_w_i[i, dr]], axis=-1)).astype(BF16),
        row(0.5 * lru_b_r[i, dr]), row(0.5 * lru_b_i[i, dr]), row(lru_lambda[i, dr]))
    conv = (conv_w[i].astype(F32), row(conv_b[i]))
    zg, ga, gb, q, k, v, hf3, xa3 = _inproj(
        h1, posf, row(mix_norm[i]), w_all, cols, row(q_norm[i]), wqm, row(kv_norm[i]),
        wk, wv, invf, sign, q_scale, (*conv, *lru_args(0)))
    ysum = _lru_bwd(xa3, hf3, *lru_args(1))

    ob = _attn(q, k, v)

    tail_weights = (
        w_lru_out[i].astype(BF16), w_mla_out[i].astype(BF16), w_o[i].astype(BF16),
        row(ffn2_norm[i]), ffn2_w_gate[i].astype(BF16), ffn2_w_up[i].astype(BF16),
        ffn2_w_down[i].astype(BF16), row(ple_norm[i]), ple_w_gate[i].astype(BF16),
        ple_w_proj[i].astype(BF16), row(ple_proj_norm[i]), row(final_norm))
    return _tail(h1, ysum, zg, ob, ga, gb, p.reshape(b, s, -1), tail_weights)
```

```python
import functools
import math

import jax
import jax.numpy as jnp
from jax import lax
from jax.experimental import pallas as pl
from jax.experimental.pallas import tpu as pltpu

F32 = jnp.float32
BF16 = jnp.bfloat16

LRU_BLOCKS = 8
CONV_WIDTH = 4
LRU_C = 8.0
MLA_HEADS = 8
QK_NOPE = 128
QK_ROPE = 64
V_HEAD = 128
ROPE_THETA = 10000.0
EPS = 1e-6

LANES = 128
SUBLANES = 8
BF16_ROWS = 16
QK_PAD = 256
STAB_LANE = QK_NOPE + QK_ROPE
DENOM_FLOOR = 2.0 ** -64
VMEM_LIMIT_BYTES = 56 * 1024 * 1024

FFN_ROWS = 512
STEPS = 64
Q_BLOCK = 512
Q_STEP_BLOCKS = 4
FF_CHUNK = 512


def _resident(shape):
    nd = len(shape)
    return pl.BlockSpec(shape, lambda *_: (0,) * nd, pipeline_mode=pl.Buffered(1))


def _params(semantics):
    return pltpu.CompilerParams(dimension_semantics=semantics,
                                vmem_limit_bytes=VMEM_LIMIT_BYTES)


def _rmsnorm(x, g):
    ms = jnp.mean(x * x, axis=-1, keepdims=True)
    return x * lax.rsqrt(ms + EPS) * g


def _dot(a, b):
    return jnp.dot(a, b, preferred_element_type=F32)


def _dot_nt(a, b):
    return lax.dot_general(a, b, (((1,), (1,)), ((), ())), preferred_element_type=F32)


def _swiglu_half(x, g, wg_ref, wu_ref, wd_ref):
    d_ff = wg_ref.shape[1]
    xn = _rmsnorm(x, g).astype(BF16)
    y = None
    for c0 in range(0, d_ff, FF_CHUNK):
        c1 = min(c0 + FF_CHUNK, d_ff)
        gate = _dot(xn, wg_ref[:, c0:c1])
        up = _dot(xn, wu_ref[:, c0:c1])
        act = (gate * jax.nn.sigmoid(gate) * up).astype(BF16)
        part = _dot(act, wd_ref[c0:c1, :])
        y = part if y is None else y + part
    return x + 0.5 * y


def _ffn_kernel(x_ref, g_ref, wg_ref, wu_ref, wd_ref, o_ref):
    o_ref[0] = _swiglu_half(x_ref[0], g_ref[...], wg_ref, wu_ref, wd_ref)


def _ffn(x, g, wg, wu, wd):
    b, s, d = x.shape
    t = FFN_ROWS
    row = pl.BlockSpec((1, t, d), lambda i, j: (i, j, 0))
    return pl.pallas_call(
        _ffn_kernel,
        out_shape=jax.ShapeDtypeStruct((b, s, d), F32),
        grid=(b, s // t),
        in_specs=[row, _resident(g.shape), _resident(wg.shape), _resident(wu.shape),
                  _resident(wd.shape)],
        out_specs=row,
        compiler_params=_params(("parallel", "parallel")),
        name="ffn",
    )(x, g, wg, wu, wd)


def _rope_tables(pos, invf, sign):
    rows = pos.shape[0]
    groups = LANES // (QK_ROPE // 2)
    q4 = rows // groups
    width = LANES // groups
    lane_grp = lax.broadcasted_iota(jnp.int32, (q4, LANES), 1) // width

    def by_group(vals):
        out = vals[groups - 1]
        for g in range(groups - 2, -1, -1):
            out = jnp.where(lane_grp == g, vals[g], out)
        return out

    ang = by_group([pos[g * q4:(g + 1) * q4, :] for g in range(groups)]) * invf
    tables = []
    for dense in (jnp.cos(ang), jnp.sin(ang)):
        rolled = [dense] + [pltpu.roll(dense, k * width, axis=1) for k in range(1, groups)]
        tables.append(jnp.concatenate(
            [by_group([rolled[(j - g) % groups] for j in range(groups)]) for g in range(groups)],
            axis=0))
    return tables[0], tables[1] * sign


def _lru_rates(lam):
    nlam = -lam
    softplus = jnp.maximum(nlam, 0.0) + jnp.log1p(jnp.exp(-jnp.abs(nlam)))
    return (-0.5 * LRU_C * math.log2(math.e)) * softplus


def _lru_gate_block(xa, n, rate, wg_ref, br_ref, bi_ref):
    blk = xa.shape[1]
    c0, c1 = n * blk, (n + 1) * blk
    gates = _dot(xa.astype(BF16), wg_ref[n])
    tr = jnp.tanh(gates[:, 0:blk] + br_ref[:, c0:c1])
    ti = jnp.tanh(gates[:, blk:2 * blk] + bi_ref[:, c0:c1])
    a = jnp.exp2(tr * rate[:, c0:c1] + rate[:, c0:c1])
    gap = 1.0 - a * a
    root = jnp.where(gap > 0.0, gap * lax.rsqrt(gap), 0.0)
    xh = 0.5 * xa
    return a, root * (ti * xh + xh)


def _lru_scan(a_s, u_s, h_s, out_ref, steps, reverse):
    def step(j, h):
        t = (steps - 1 - j) if reverse else j
        r0 = pl.multiple_of(t * SUBLANES, SUBLANES)
        h = a_s[:, pl.ds(r0, SUBLANES), :] * h + u_s[:, pl.ds(r0, SUBLANES), :]
        out_ref[:, pl.ds(r0, SUBLANES), :] = h
        return h

    h_s[...] = lax.fori_loop(0, steps, step, h_s[...], unroll=8)


def _inproj_kernel(h_ref, hp_ref, hn_ref, pos_ref, g_ref, w0_ref, w1_ref, w2_ref, w3_ref, qg_ref,
                   wqm_ref, kvg_ref, wkv_ref, invf_ref, sign_ref, cw_ref, cb_ref, wgl_ref,
                   br_ref, bi_ref, lam_ref,
                   zg_ref, ga_ref, gb_ref, q_ref, k_ref, v_ref, hf_ref, xa_ref,
                   xpad, a_s, u_s, h_s, hf_s, *, cols, q_scale):
    i = pl.program_id(0)
    nb, steps, d = h_ref.shape
    rows = nb * steps
    halo = hp_ref.shape[1]
    un = _rmsnorm(h_ref[...].reshape(rows, d), g_ref[...]).astype(BF16)

    def proj(name, lhs=un):
        part, c0, c1 = cols[name]
        return _dot(lhs, (w0_ref, w1_ref, w2_ref, w3_ref)[part][:, c0:c1])

    def per_batch(val):
        return val.reshape(nb, steps, val.shape[-1])

    @pl.when(i == 0)
    def _():
        h_s[...] = jnp.zeros_like(h_s)

    zx = proj("zx")
    h_halo = jnp.concatenate([hp_ref[...].reshape(nb * halo, d), hn_ref[...].reshape(nb * halo, d)])
    zx_halo = proj("zx", lhs=_rmsnorm(h_halo, g_ref[...]).astype(BF16))
    zx_prev = jnp.where(i > 0, zx_halo[0:nb * halo], 0.0)
    zx_next = jnp.where(i < pl.num_programs(0) - 1, zx_halo[nb * halo:], 0.0)
    first = halo * nb
    for n in range(LRU_BLOCKS):
        lanes = slice(n * LANES, (n + 1) * LANES)
        for b in range(nb):
            xpad[n, pl.ds(b, halo, stride=nb), :] = zx_prev[b * halo:(b + 1) * halo, lanes]
            xpad[n, pl.ds(first + b, steps, stride=nb), :] = zx[b * steps:(b + 1) * steps, lanes]
            xpad[n, pl.ds(first + rows + b, halo, stride=nb), :] = (
                zx_next[b * halo:(b + 1) * halo, lanes])

    rate = _lru_rates(lam_ref[...])

    def lru_block(n):
        lanes = slice(n * LANES, (n + 1) * LANES)
        xa = cb_ref[:, lanes]
        for k in range(CONV_WIDTH):
            r0 = first + (k - CONV_WIDTH // 2) * SUBLANES
            xa = xa + xpad[n, r0:r0 + rows, :] * cw_ref[k:k + 1, lanes]
        xa_ref[n] = xa.astype(xa_ref.dtype)
        a_s[n], u_s[n] = _lru_gate_block(xa, n, rate, wgl_ref, br_ref, bi_ref)

    def half_proj(name, out_ref, half):
        part, c0, c1 = cols[name]
        mid = (c0 + c1) // 2
        lo, hi = (c0, mid) if half == 0 else (mid, c1)
        w = (w0_ref, w1_ref, w2_ref, w3_ref)[part]
        out_ref[:, :, lo - c0:hi - c0] = per_batch(_dot(un, w[:, lo:hi]).astype(BF16))

    gate_outputs = [(name, ref, half) for name, ref in (("zg", zg_ref), ("ga", ga_ref),
                                                        ("gb", gb_ref)) for half in (0, 1)]
    for n, gate_output in enumerate(gate_outputs):
        lru_block(n)
        half_proj(*gate_output)
    first_late = len(gate_outputs)
    lru_block(first_late)

    cosv, sinv = _rope_tables(pos_ref[...].reshape(rows, 1), invf_ref[...], sign_ref[...])
    lane = lax.broadcasted_iota(jnp.int32, (rows, LANES), 1)
    low = lane < QK_ROPE
    table = jnp.where(low, cosv, sinv)

    def rotate(x, tab):
        y = x * tab
        return jnp.where(low, y + pltpu.roll(y, QK_ROPE, axis=1), 0.0)

    cqn = _rmsnorm(proj("cq"), qg_ref[...]).astype(BF16)
    qm = _dot(cqn, wqm_ref[...])
    q_table = table * q_scale
    for h in range(MLA_HEADS):
        m0 = h * QK_PAD
        q_ref[:, h, :, 0:QK_NOPE] = per_batch((qm[:, m0:m0 + QK_NOPE] * q_scale).astype(BF16))
        rope = rotate(qm[:, m0 + QK_NOPE:m0 + QK_PAD], q_table)
        q_ref[:, h, :, QK_NOPE:QK_PAD] = per_batch(rope.astype(BF16))

    for n in range(first_late + 1, LRU_BLOCKS):
        lru_block(n)

    ckvn =_rmsnorm(proj("ckv"), kvg_ref[...]).astype(BF16)
    kv = _dot(ckvn, wkv_ref[...])
    kpe = per_batch(jnp.where(lane == STAB_LANE - QK_NOPE, 1.0,
                              rotate(proj("kr"), table)).astype(BF16))
    for h in range(MLA_HEADS):
        c0 = h * (QK_NOPE + V_HEAD)
        k_ref[:, h, :, 0:QK_NOPE] = per_batch(kv[:, c0:c0 + QK_NOPE].astype(BF16))
        k_ref[:, h, :, QK_NOPE:QK_PAD] = kpe
        v_ref[:, h, :, :] = per_batch(kv[:, c0 + QK_NOPE:c0 + QK_NOPE + V_HEAD].astype(BF16))

    _lru_scan(a_s, u_s, h_s, hf_s, steps, reverse=False)
    hf_ref[...] = hf_s[...].astype(hf_ref.dtype)


def _inproj(h1, posf, g, w_parts, cols, qg, wqm, kvg, wkv, invf, sign, q_scale, lru_weights):
    b, s, d = h1.shape
    t = STEPS
    halo = SUBLANES
    width = cols["zx"][2] - cols["zx"][1]
    nblk = width // LANES
    rows = t * b
    bm = lambda x: pl.BlockSpec((b, t, x), lambda i: (0, i, 0))
    heads = lambda x: pl.BlockSpec((b, MLA_HEADS, t, x), lambda i: (0, 0, i, 0))
    tm = pl.BlockSpec((nblk, rows, LANES), lambda i: (0, i, 0))
    per_halo = t // halo
    prev = pl.BlockSpec((b, halo, d), lambda i: (0, jnp.maximum(i * per_halo - 1, 0), 0))
    nxt = pl.BlockSpec((b, halo, d), lambda i: (0, jnp.minimum((i + 1) * per_halo,
                                                               s // halo - 1), 0))
    weights = (g, *w_parts, qg, wqm, kvg, wkv, invf, sign, *lru_weights)
    kernel = functools.partial(_inproj_kernel, cols=cols, q_scale=q_scale)
    return pl.pallas_call(
        kernel,
        out_shape=[
            jax.ShapeDtypeStruct((b, s, width), BF16),
            jax.ShapeDtypeStruct((b, s, d), BF16),
            jax.ShapeDtypeStruct((b, s, d), BF16),
            jax.ShapeDtypeStruct((b, MLA_HEADS, s, QK_PAD), BF16),
            jax.ShapeDtypeStruct((b, MLA_HEADS, s, QK_PAD), BF16),
            jax.ShapeDtypeStruct((b, MLA_HEADS, s, V_HEAD), BF16),
            jax.ShapeDtypeStruct((nblk, s * b, LANES), BF16),
            jax.ShapeDtypeStruct((nblk, s * b, LANES), BF16),
        ],
        grid=(s // t,),
        in_specs=[bm(d), prev, nxt, bm(1)] + [_resident(a.shape) for a in weights],
        out_specs=[bm(width), bm(d), bm(d), heads(QK_PAD), heads(QK_PAD), heads(V_HEAD), tm, tm],
        scratch_shapes=[
            pltpu.VMEM((nblk, rows + 2 * halo * b, LANES), F32),
            pltpu.VMEM((nblk, rows, LANES), F32),
            pltpu.VMEM((nblk, rows, LANES), F32),
            pltpu.VMEM((nblk, SUBLANES, LANES), F32),
            pltpu.VMEM((nblk, rows, LANES), F32),
        ],
        compiler_params=_params(("arbitrary",)),
        name="inproj",
    )(h1, h1, h1, posf, *weights)


def _lru_bwd_kernel(xa_ref, hf_ref, wg_ref, br_ref, bi_ref, lam_ref, o_ref, a_s, u_s, h_s, hb_s,
                    *, steps):
    @pl.when(pl.program_id(0) == 0)
    def _():
        h_s[...] = jnp.zeros_like(h_s)

    rate = _lru_rates(lam_ref[...])
    for n in range(LRU_BLOCKS):
        a_s[n], u_s[n] = _lru_gate_block(xa_ref[n].astype(F32), n, rate, wg_ref, br_ref, bi_ref)
    _lru_scan(a_s, u_s, h_s, hb_s, steps, reverse=True)

    hb_s[...] = hb_s[...] + hf_ref[...].astype(F32)
    blk = xa_ref.shape[2]
    for n in range(LRU_BLOCKS):
        for b in range(SUBLANES):
            o_ref[b, :, n * blk:(n + 1) * blk] = (
                hb_s[n, pl.ds(b, steps, stride=SUBLANES), :].astype(o_ref.dtype))


def _lru_bwd(xa3, hf3, wg, br, bi, lam):
    nblk, n_rows, blk = xa3.shape
    steps = STEPS
    rows = steps * SUBLANES
    n_tiles = n_rows // rows
    cur = pl.BlockSpec((nblk, rows, blk), lambda i: (0, n_tiles - 1 - i, 0))
    weights = (wg, br, bi, lam)
    return pl.pallas_call(
        functools.partial(_lru_bwd_kernel, steps=steps),
        out_shape=jax.ShapeDtypeStruct((SUBLANES, n_rows // SUBLANES, nblk * blk), BF16),
        grid=(n_tiles,),
        in_specs=[cur, cur] + [_resident(a.shape) for a in weights],
        out_specs=pl.BlockSpec((SUBLANES, steps, nblk * blk), lambda i: (0, n_tiles - 1 - i, 0)),
        scratch_shapes=[
            pltpu.VMEM((nblk, rows, blk), F32),
            pltpu.VMEM((nblk, rows, blk), F32),
            pltpu.VMEM((nblk, SUBLANES, blk), F32),
            pltpu.VMEM((nblk, rows, blk), F32),
        ],
        compiler_params=_params(("arbitrary",)),
        name="lru_bwd",
    )(xa3, hf3, *weights)


def _attn_kernel(q_ref, k_ref, v_ref, o_ref, p_s, vt_s, kmax_s):
    dv = v_ref.shape[-1]
    tb = p_s.shape[-1]
    n_blocks = q_ref.shape[2] // tb
    lane = lax.broadcasted_iota(jnp.int32, (1, q_ref.shape[-1]), 1)

    @pl.when(pl.program_id(2) == 0)
    def _():
        vt_s[0:dv, :] = v_ref[0, 0].astype(F32).T.astype(BF16)
        row = lax.broadcasted_iota(jnp.int32, (BF16_ROWS, vt_s.shape[1]), 0)
        vt_s[dv:dv + BF16_ROWS, :] = jnp.where(row == 0, 1.0, 0.0).astype(BF16)
        kf = k_ref[0, 0].astype(F32)
        ksq = jnp.max(jnp.sum(kf * kf, axis=1, keepdims=True), axis=0, keepdims=True)
        kmax_s[...] = jnp.broadcast_to(jnp.sqrt(ksq), kmax_s.shape)

    def block(i, exact):
        r0 = i * tb
        slot = i % p_s.shape[0]
        qf = q_ref[0, 0, r0:r0 + tb, :].astype(F32)
        bound = jnp.sqrt(jnp.sum(qf * qf, axis=1, keepdims=True)) * kmax_s[...]
        qs = jnp.where(lane == STAB_LANE, -bound, qf).astype(BF16)
        st = _dot_nt(k_ref[0, 0], qs)
        if exact:
            st = st - jnp.max(st, axis=0, keepdims=True)
        p_s[slot] = jnp.exp2(st).astype(BF16)
        ot = _dot(vt_s[...], p_s[slot])
        denom = ot[dv:dv + 1, :]
        o_ref[0, r0:r0 + tb, :] = (ot[0:dv, :] / denom).T.astype(o_ref.dtype)
        return jnp.min(denom)

    smallest = block(0, exact=False)
    for i in range(1, n_blocks):
        smallest = jnp.minimum(smallest, block(i, exact=False))

    @pl.when(jnp.logical_not(smallest >= DENOM_FLOOR))
    def _():
        for i in range(n_blocks):
            block(i, exact=True)


def _attn(q, k, v):
    b, nh, s, dq = q.shape
    dv = v.shape[-1]
    tq = min(Q_STEP_BLOCKS * Q_BLOCK, s)
    head = lambda x: pl.BlockSpec((1, 1, s, x), lambda i, h, j: (i, h, 0, 0))
    return pl.pallas_call(
        _attn_kernel,
        out_shape=jax.ShapeDtypeStruct((b, s, nh * dv), BF16),
        grid=(b, nh, s // tq),
        in_specs=[pl.BlockSpec((1, 1, tq, dq), lambda i, h, j: (i, h, j, 0)), head(dq), head(dv)],
        out_specs=pl.BlockSpec((1, tq, dv), lambda i, h, j: (i, j, h)),
        scratch_shapes=[
            pltpu.VMEM((2, s, Q_BLOCK), BF16),
            pltpu.VMEM((dv + BF16_ROWS, s), BF16),
            pltpu.VMEM((1, dq), F32),
        ],
        compiler_params=_params(("parallel", "parallel", "arbitrary")),
        name="attn",
    )(q, k, v)


def _tail_kernel(h_ref, ys_ref, zg_ref, ob_ref, ga_ref, gb_ref, p_ref, wl_ref, wm_ref, wo_ref,
                 g_ref, wg_ref, wu_ref, wd_ref, pg_ref, wpg_ref, wpp_ref, ppg_ref, fg_ref, o_ref):
    ya_in = ys_ref[0].astype(F32) * jax.nn.gelu(zg_ref[0].astype(F32))
    ya = _dot(ya_in.astype(BF16), wl_ref[...])
    yb = _dot(ob_ref[0], wm_ref[...])
    merged = (jax.nn.sigmoid(ga_ref[0].astype(F32)) * ya
              + jax.nn.sigmoid(gb_ref[0].astype(F32)) * yb)
    h = h_ref[0] + _dot(merged.astype(BF16), wo_ref[...])

    h = _swiglu_half(h, g_ref[...], wg_ref, wu_ref, wd_ref)
    gate = jax.nn.sigmoid(_dot(_rmsnorm(h, pg_ref[...]).astype(BF16), wpg_ref[...]))
    emb = _rmsnorm(_dot(p_ref[0].astype(BF16), wpp_ref[...]), ppg_ref[...])
    h = h + gate * emb
    o_ref[0] = _rmsnorm(h, fg_ref[...])


def _tail(h1, ysum, zg, ob, ga, gb, p, weights):
    b, s, d = h1.shape
    t = FFN_ROWS
    row = pl.BlockSpec((1, t, d), lambda i, j: (i, j, 0))
    prow = pl.BlockSpec((1, t, p.shape[-1]), lambda i, j: (i, j, 0))
    return pl.pallas_call(
        _tail_kernel,
        out_shape=jax.ShapeDtypeStruct((b, s, d), F32),
        grid=(b, s // t),
        in_specs=[row] * 6 + [prow] + [_resident(a.shape) for a in weights],
        out_specs=row,
        compiler_params=_params(("parallel", "parallel")),
        name="tail",
    )(h1, ysum, zg, ob, ga, gb, p, *weights)


def _swap_halves(w):
    half = w.shape[-1] // 2
    return jnp.concatenate([w[..., half:], w[..., :half]], axis=-1)


def _in_weights(w_in, d, lru_w, q_lora, kv_lora):
    sizes = (lru_w, lru_w, q_lora, kv_lora, QK_ROPE, d, d)
    offs = [0]
    for sz in sizes:
        offs.append(offs[-1] + sz)
    kr = w_in[:, offs[4]:offs[5]]
    pieces = (w_in[:, :offs[4]], jnp.concatenate([kr, _swap_halves(kr)], axis=1),
              w_in[:, offs[5]:offs[6]], w_in[:, offs[6]:offs[7]])
    cols = {name: (0, offs[i], offs[i + 1]) for i, name in enumerate(("zx", "zg", "cq", "ckv"))}
    cols.update(kr=(1, 0, 2 * QK_ROPE), ga=(2, 0, d), gb=(3, 0, d))
    return tuple(piece.astype(BF16) for piece in pieces), cols


def _q_weights(w_uq):
    r = w_uq.shape[0]
    w = w_uq.reshape(r, MLA_HEADS, QK_NOPE + QK_ROPE)
    full = jnp.concatenate([w, _swap_halves(w[:, :, QK_NOPE:])], axis=-1)
    assert full.shape[-1] == QK_PAD
    return full.reshape(r, MLA_HEADS * QK_PAD).astype(BF16)


def kernel(x, p, positions, ffn1_norm, ffn1_w_gate, ffn1_w_up, ffn1_w_down, mix_norm, w_in, conv_w, conv_b, lru_w_r, lru_b_r, lru_w_i, lru_b_i, lru_lambda, w_lru_out, q_norm, w_uq, kv_norm, w_ukv, w_mla_out, w_o, ffn2_norm, ffn2_w_gate, ffn2_w_up, ffn2_w_down, ple_norm, ple_w_gate, ple_w_proj, ple_proj_norm, final_norm):
    b, s, d = x.shape
    depth = ffn1_norm.shape[0]
    lru_w = conv_w.shape[-1]
    q_lora, kv_lora = q_norm.shape[-1], kv_norm.shape[-1]
    assert b == SUBLANES and lru_w == d
    assert depth == 1
    assert s % FFN_ROWS == 0 and s % STEPS == 0
    assert s % Q_BLOCK == 0 and s % min(Q_STEP_BLOCKS * Q_BLOCK, s) == 0

    row = lambda v: v.reshape(1, -1).astype(F32)
    posf = positions.astype(F32).reshape(b, s, 1)
    half = QK_ROPE // 2
    inv_freq = ROPE_THETA ** (-jnp.arange(0, QK_ROPE, 2, dtype=F32) / QK_ROPE)
    invf = jnp.tile(inv_freq, LANES // half).reshape(1, LANES)
    sign = jnp.tile(jnp.concatenate([-jnp.ones(half, F32), jnp.ones(half, F32)]),
                    LANES // QK_ROPE).reshape(1, LANES)
    q_scale = (QK_NOPE + QK_ROPE) ** -0.5 * math.log2(math.e)

    i = 0
    h1 = _ffn(x, row(ffn1_norm[i]), ffn1_w_gate[i].astype(BF16), ffn1_w_up[i].astype(BF16),
              ffn1_w_down[i].astype(BF16))

    w_parts, cols = _in_weights(w_in[i], d, lru_w, q_lora, kv_lora)
    wqm = _q_weights(w_uq[i])
    lru_args = lambda dr: (
        (0.5 * jnp.concatenate([lru_w_r[i, dr], lru_w_i[i, dr]], axis=-1)).astype(BF16),
        row(0.5 * lru_b_r[i, dr]), row(0.5 * lru_b_i[i, dr]), row(lru_lambda[i, dr]))
    conv = (conv_w[i].astype(F32), row(conv_b[i]))
    zg, ga, gb, q, k, v, hf3, xa3 = _inproj(
        h1, posf, row(mix_norm[i]), w_parts, cols, row(q_norm[i]), wqm, row(kv_norm[i]),
        w_ukv[i].astype(BF16), invf, sign, q_scale, (*conv, *lru_args(0)))
    ysum = _lru_bwd(xa3, hf3, *lru_args(1))

    ob = _attn(q, k, v)

    tail_weights = (
        w_lru_out[i].astype(BF16), w_mla_out[i].astype(BF16), w_o[i].astype(BF16),
        row(ffn2_norm[i]), ffn2_w_gate[i].astype(BF16), ffn2_w_up[i].astype(BF16),
        ffn2_w_down[i].astype(BF16), row(ple_norm[i]), ple_w_gate[i].astype(BF16),
        ple_w_proj[i].astype(BF16), row(ple_proj_norm[i]), row(final_norm))
    return _tail(h1, ysum, zg, ob, ga, gb, p.reshape(b, s, -1), tail_weights)
```

```python
import functools
import math

import jax
import jax.numpy as jnp
from jax import lax
from jax.experimental import pallas as pl
from jax.experimental.pallas import tpu as pltpu

F32 = jnp.float32
BF16 = jnp.bfloat16

LRU_BLOCKS = 8
CONV_WIDTH = 4
LRU_C = 8.0
MLA_HEADS = 8
QK_NOPE = 128
QK_ROPE = 64
V_HEAD = 128
ROPE_THETA = 10000.0
EPS = 1e-6

LANES = 128
SUBLANES = 8
BF16_ROWS = 16
QK_PAD = 256
STAB_LANE = QK_NOPE + QK_ROPE
DENOM_FLOOR = 2.0 ** -64
VMEM_LIMIT_BYTES = 56 * 1024 * 1024

FFN_ROWS = 512
STEPS = 64
BWD_STEPS = 128
Q_BLOCK = 512
Q_STEP_BLOCKS = 4
FF_CHUNK = 512


def _resident(shape):
    nd = len(shape)
    return pl.BlockSpec(shape, lambda *_: (0,) * nd, pipeline_mode=pl.Buffered(1))


def _params(semantics):
    return pltpu.CompilerParams(dimension_semantics=semantics,
                                vmem_limit_bytes=VMEM_LIMIT_BYTES)


def _rmsnorm(x, g):
    ms = jnp.mean(x * x, axis=-1, keepdims=True)
    return x * lax.rsqrt(ms + EPS) * g


def _dot(a, b):
    return jnp.dot(a, b, preferred_element_type=F32)


def _dot_nt(a, b):
    return lax.dot_general(a, b, (((1,), (1,)), ((), ())), preferred_element_type=F32)


def _swiglu_half(x, g, wg_ref, wu_ref, wd_ref):
    d_ff = wg_ref.shape[1]
    xn = _rmsnorm(x, g).astype(BF16)
    y = None
    for c0 in range(0, d_ff, FF_CHUNK):
        c1 = min(c0 + FF_CHUNK, d_ff)
        gate = _dot(xn, wg_ref[:, c0:c1])
        up = _dot(xn, wu_ref[:, c0:c1])
        act = (gate * jax.nn.sigmoid(gate) * up).astype(BF16)
        part = _dot(act, wd_ref[c0:c1, :])
        y = part if y is None else y + part
    return x + 0.5 * y


def _ffn_kernel(x_ref, g_ref, wg_ref, wu_ref, wd_ref, o_ref):
    o_ref[0] = _swiglu_half(x_ref[0], g_ref[...], wg_ref, wu_ref, wd_ref)


def _ffn(x, g, wg, wu, wd):
    b, s, d = x.shape
    t = FFN_ROWS
    row = pl.BlockSpec((1, t, d), lambda i, j: (i, j, 0))
    return pl.pallas_call(
        _ffn_kernel,
        out_shape=jax.ShapeDtypeStruct((b, s, d), F32),
        grid=(b, s // t),
        in_specs=[row, _resident(g.shape), _resident(wg.shape), _resident(wu.shape),
                  _resident(wd.shape)],
        out_specs=row,
        compiler_params=_params(("parallel", "parallel")),
        name="ffn",
    )(x, g, wg, wu, wd)


ROPE_GROUPS = LANES // (QK_ROPE // 2)


def _dense_positions(positions, steps):
    b, s = positions.shape
    tiles = s // steps
    rows = b * steps
    per_tile = positions.astype(F32).reshape(b, tiles, steps).transpose(1, 0, 2)
    grouped = per_tile.reshape(tiles, ROPE_GROUPS, rows // ROPE_GROUPS).transpose(0, 2, 1)
    return jnp.repeat(grouped, LANES // ROPE_GROUPS, axis=2)


def _rope_tables(dense_pos, invf, sign):
    groups = ROPE_GROUPS
    q4 = dense_pos.shape[0]
    width = LANES // groups
    lane_grp = lax.broadcasted_iota(jnp.int32, (q4, LANES), 1) // width

    def by_group(vals):
        out = vals[groups - 1]
        for g in range(groups - 2, -1, -1):
            out = jnp.where(lane_grp == g, vals[g], out)
        return out

    ang = dense_pos * invf
    tables = []
    for dense in (jnp.cos(ang), jnp.sin(ang)):
        rolled = [dense] + [pltpu.roll(dense, k * width, axis=1) for k in range(1, groups)]
        tables.append(jnp.concatenate(
            [by_group([rolled[(j - g) % groups] for j in range(groups)]) for g in range(groups)],
            axis=0))
    return tables[0], tables[1] * sign


def _lru_rates(lam):
    nlam = -lam
    softplus = jnp.maximum(nlam, 0.0) + jnp.log1p(jnp.exp(-jnp.abs(nlam)))
    return (-0.5 * LRU_C * math.log2(math.e)) * softplus


def _lru_gate_block(xa, n, rate, wg_ref, br_ref, bi_ref):
    blk = xa.shape[1]
    c0, c1 = n * blk, (n + 1) * blk
    gates = _dot(xa.astype(BF16), wg_ref[n])
    tr = jnp.tanh(gates[:, 0:blk] + br_ref[:, c0:c1])
    ti = jnp.tanh(gates[:, blk:2 * blk] + bi_ref[:, c0:c1])
    a = jnp.exp2(tr * rate[:, c0:c1] + rate[:, c0:c1])
    gap = 1.0 - a * a
    root = jnp.where(gap > 0.0, gap * lax.rsqrt(gap), 0.0)
    xh = 0.5 * xa
    return a, root * (ti * xh + xh)


def _lru_scan(a_s, u_s, h_s, out_ref, steps, reverse):
    def step(j, h):
        t = (steps - 1 - j) if reverse else j
        r0 = pl.multiple_of(t * SUBLANES, SUBLANES)
        h = a_s[:, pl.ds(r0, SUBLANES), :] * h + u_s[:, pl.ds(r0, SUBLANES), :]
        out_ref[:, pl.ds(r0, SUBLANES), :] = h
        return h

    h_s[...] = lax.fori_loop(0, steps, step, h_s[...], unroll=8)


def _inproj_kernel(h_ref, hp_ref, hn_ref, pos_ref, g_ref, w0_ref, w1_ref, w2_ref, w3_ref, qg_ref,
                   wqm_ref, kvg_ref, wkv_ref, invf_ref, sign_ref, cw_ref, cb_ref, wgl_ref,
                   br_ref, bi_ref, lam_ref,
                   zg_ref, ga_ref, gb_ref, q_ref, k_ref, v_ref, hf_ref, xa_ref,
                   xpad, a_s, u_s, h_s, hf_s, *, cols, q_scale):
    i = pl.program_id(0)
    nb, steps, d = h_ref.shape
    rows = nb * steps
    halo = hp_ref.shape[1]
    un = _rmsnorm(h_ref[...].reshape(rows, d), g_ref[...]).astype(BF16)

    def proj(name, lhs=un):
        part, c0, c1 = cols[name]
        return _dot(lhs, (w0_ref, w1_ref, w2_ref, w3_ref)[part][:, c0:c1])

    def per_batch(val):
        return val.reshape(nb, steps, val.shape[-1])

    @pl.when(i == 0)
    def _():
        h_s[...] = jnp.zeros_like(h_s)

    zx = proj("zx")
    h_halo = jnp.concatenate([hp_ref[...].reshape(nb * halo, d), hn_ref[...].reshape(nb * halo, d)])
    zx_halo = proj("zx", lhs=_rmsnorm(h_halo, g_ref[...]).astype(BF16))
    zx_prev = jnp.where(i > 0, zx_halo[0:nb * halo], 0.0)
    zx_next = jnp.where(i < pl.num_programs(0) - 1, zx_halo[nb * halo:], 0.0)
    first = halo * nb
    for n in range(LRU_BLOCKS):
        lanes = slice(n * LANES, (n + 1) * LANES)
        for b in range(nb):
            xpad[n, pl.ds(b, halo, stride=nb), :] = zx_prev[b * halo:(b + 1) * halo, lanes]
            xpad[n, pl.ds(first + b, steps, stride=nb), :] = zx[b * steps:(b + 1) * steps, lanes]
            xpad[n, pl.ds(first + rows + b, halo, stride=nb), :] = (
                zx_next[b * halo:(b + 1) * halo, lanes])

    rate = _lru_rates(lam_ref[...])

    def lru_block(n):
        lanes = slice(n * LANES, (n + 1) * LANES)
        xa = cb_ref[:, lanes]
        for k in range(CONV_WIDTH):
            r0 = first + (k - CONV_WIDTH // 2) * SUBLANES
            xa = xa + xpad[n, r0:r0 + rows, :] * cw_ref[k:k + 1, lanes]
        xa_ref[n] = xa.astype(xa_ref.dtype)
        a_s[n], u_s[n] = _lru_gate_block(xa, n, rate, wgl_ref, br_ref, bi_ref)

    def half_proj(name, out_ref, half):
        part, c0, c1 = cols[name]
        mid = (c0 + c1) // 2
        lo, hi = (c0, mid) if half == 0 else (mid, c1)
        w = (w0_ref, w1_ref, w2_ref, w3_ref)[part]
        out_ref[:, :, lo - c0:hi - c0] = per_batch(_dot(un, w[:, lo:hi]).astype(BF16))

    gate_outputs = [(name, ref, half) for name, ref in (("zg", zg_ref), ("ga", ga_ref),
                                                        ("gb", gb_ref)) for half in (0, 1)]
    for n, gate_output in enumerate(gate_outputs):
        lru_block(n)
        half_proj(*gate_output)
    first_late = len(gate_outputs)
    lru_block(first_late)

    cosv, sinv = _rope_tables(pos_ref[0], invf_ref[...], sign_ref[...])
    lane = lax.broadcasted_iota(jnp.int32, (rows, LANES), 1)
    low = lane < QK_ROPE
    table = jnp.where(low, cosv, sinv)

    def rotate(x, tab):
        y = x * tab
        return jnp.where(low, y + pltpu.roll(y, QK_ROPE, axis=1), 0.0)

    cqn = _rmsnorm(proj("cq"), qg_ref[...]).astype(BF16)
    qm = _dot(cqn, wqm_ref[...])
    q_table = table * q_scale
    for h in range(MLA_HEADS):
        m0 = h * QK_PAD
        q_ref[:, h, :, 0:QK_NOPE] = per_batch((qm[:, m0:m0 + QK_NOPE] * q_scale).astype(BF16))
        rope = rotate(qm[:, m0 + QK_NOPE:m0 + QK_PAD], q_table)
        q_ref[:, h, :, QK_NOPE:QK_PAD] = per_batch(rope.astype(BF16))

    for n in range(first_late + 1, LRU_BLOCKS):
        lru_block(n)

    ckvn =_rmsnorm(proj("ckv"), kvg_ref[...]).astype(BF16)
    kv = _dot(ckvn, wkv_ref[...])
    kpe = per_batch(jnp.where(lane == STAB_LANE - QK_NOPE, 1.0,
                              rotate(proj("kr"), table)).astype(BF16))
    for h in range(MLA_HEADS):
        c0 = h * (QK_NOPE + V_HEAD)
        k_ref[:, h, :, 0:QK_NOPE] = per_batch(kv[:, c0:c0 + QK_NOPE].astype(BF16))
        k_ref[:, h, :, QK_NOPE:QK_PAD] = kpe
        v_ref[:, h, :, :] = per_batch(kv[:, c0 + QK_NOPE:c0 + QK_NOPE + V_HEAD].astype(BF16))

    _lru_scan(a_s, u_s, h_s, hf_s, steps, reverse=False)
    hf_ref[...] = hf_s[...].astype(hf_ref.dtype)


def _inproj(h1, posf, g, w_parts, cols, qg, wqm, kvg, wkv, invf, sign, q_scale, lru_weights):
    b, s, d = h1.shape
    t = STEPS
    halo = SUBLANES
    width = cols["zx"][2] - cols["zx"][1]
    nblk = width // LANES
    rows = t * b
    bm = lambda x: pl.BlockSpec((b, t, x), lambda i: (0, i, 0))
    heads = lambda x: pl.BlockSpec((b, MLA_HEADS, t, x), lambda i: (0, 0, i, 0))
    tm = pl.BlockSpec((nblk, rows, LANES), lambda i: (0, i, 0))
    per_halo = t // halo
    prev = pl.BlockSpec((b, halo, d), lambda i: (0, jnp.maximum(i * per_halo - 1, 0), 0))
    nxt = pl.BlockSpec((b, halo, d), lambda i: (0, jnp.minimum((i + 1) * per_halo,
                                                               s // halo - 1), 0))
    weights = (g, *w_parts, qg, wqm, kvg, wkv, invf, sign, *lru_weights)
    kernel = functools.partial(_inproj_kernel, cols=cols, q_scale=q_scale)
    return pl.pallas_call(
        kernel,
        out_shape=[
            jax.ShapeDtypeStruct((b, s, width), BF16),
            jax.ShapeDtypeStruct((b, s, d), BF16),
            jax.ShapeDtypeStruct((b, s, d), BF16),
            jax.ShapeDtypeStruct((b, MLA_HEADS, s, QK_PAD), BF16),
            jax.ShapeDtypeStruct((b, MLA_HEADS, s, QK_PAD), BF16),
            jax.ShapeDtypeStruct((b, MLA_HEADS, s, V_HEAD), BF16),
            jax.ShapeDtypeStruct((nblk, s * b, LANES), BF16),
            jax.ShapeDtypeStruct((nblk, s * b, LANES), BF16),
        ],
        grid=(s // t,),
        in_specs=[bm(d), prev, nxt, pl.BlockSpec((1,) + posf.shape[1:], lambda i: (i, 0, 0))]
        + [_resident(a.shape) for a in weights],
        out_specs=[bm(width), bm(d), bm(d), heads(QK_PAD), heads(QK_PAD), heads(V_HEAD), tm, tm],
        scratch_shapes=[
            pltpu.VMEM((nblk, rows + 2 * halo * b, LANES), F32),
            pltpu.VMEM((nblk, rows, LANES), F32),
            pltpu.VMEM((nblk, rows, LANES), F32),
            pltpu.VMEM((nblk, SUBLANES, LANES), F32),
            pltpu.VMEM((nblk, rows, LANES), F32),
        ],
        compiler_params=_params(("arbitrary",)),
        name="inproj",
    )(h1, h1, h1, posf, *weights)


def _lru_bwd_kernel(xa_ref, hf_ref, wg_ref, br_ref, bi_ref, lam_ref, o_ref, a_s, u_s, h_s, hb_s,
                    *, steps):
    @pl.when(pl.program_id(0) == 0)
    def _():
        h_s[...] = jnp.zeros_like(h_s)

    rate = _lru_rates(lam_ref[...])
    for n in range(LRU_BLOCKS):
        a_s[n], u_s[n] = _lru_gate_block(xa_ref[n].astype(F32), n, rate, wg_ref, br_ref, bi_ref)
    _lru_scan(a_s, u_s, h_s, hb_s, steps, reverse=True)

    hb_s[...] = hb_s[...] + hf_ref[...].astype(F32)
    blk = xa_ref.shape[2]
    for n in range(LRU_BLOCKS):
        for b in range(SUBLANES):
            o_ref[b, :, n * blk:(n + 1) * blk] = (
                hb_s[n, pl.ds(b, steps, stride=SUBLANES), :].astype(o_ref.dtype))


def _lru_bwd(xa3, hf3, wg, br, bi, lam):
    nblk, n_rows, blk = xa3.shape
    steps = BWD_STEPS
    rows = steps * SUBLANES
    n_tiles = n_rows // rows
    cur = pl.BlockSpec((nblk, rows, blk), lambda i: (0, n_tiles - 1 - i, 0))
    weights = (wg, br, bi, lam)
    return pl.pallas_call(
        functools.partial(_lru_bwd_kernel, steps=steps),
        out_shape=jax.ShapeDtypeStruct((SUBLANES, n_rows // SUBLANES, nblk * blk), BF16),
        grid=(n_tiles,),
        in_specs=[cur, cur] + [_resident(a.shape) for a in weights],
        out_specs=pl.BlockSpec((SUBLANES, steps, nblk * blk), lambda i: (0, n_tiles - 1 - i, 0)),
        scratch_shapes=[
            pltpu.VMEM((nblk, rows, blk), F32),
            pltpu.VMEM((nblk, rows, blk), F32),
            pltpu.VMEM((nblk, SUBLANES, blk), F32),
            pltpu.VMEM((nblk, rows, blk), F32),
        ],
        compiler_params=_params(("arbitrary",)),
        name="lru_bwd",
    )(xa3, hf3, *weights)


def _attn_kernel(q_ref, k_ref, v_ref, o_ref, p_s, vt_s, kmax_s):
    dv = v_ref.shape[-1]
    tb = p_s.shape[-1]
    n_blocks = q_ref.shape[2] // tb
    lane = lax.broadcasted_iota(jnp.int32, (1, q_ref.shape[-1]), 1)

    @pl.when(pl.program_id(2) == 0)
    def _():
        vt_s[0:dv, :] = v_ref[0, 0].astype(F32).T.astype(BF16)
        row = lax.broadcasted_iota(jnp.int32, (BF16_ROWS, vt_s.shape[1]), 0)
        vt_s[dv:dv + BF16_ROWS, :] = jnp.where(row == 0, 1.0, 0.0).astype(BF16)
        kf = k_ref[0, 0].astype(F32)
        ksq = jnp.max(jnp.sum(kf * kf, axis=1, keepdims=True), axis=0, keepdims=True)
        kmax_s[...] = jnp.broadcast_to(jnp.sqrt(ksq), kmax_s.shape)

    def block(i, exact):
        r0 = i * tb
        slot = i % p_s.shape[0]
        qf = q_ref[0, 0, r0:r0 + tb, :].astype(F32)
        bound = jnp.sqrt(jnp.sum(qf * qf, axis=1, keepdims=True)) * kmax_s[...]
        qs = jnp.where(lane == STAB_LANE, -bound, qf).astype(BF16)
        st = _dot_nt(k_ref[0, 0], qs)
        if exact:
            st = st - jnp.max(st, axis=0, keepdims=True)
        p_s[slot] = jnp.exp2(st).astype(BF16)
        ot = _dot(vt_s[...], p_s[slot])
        denom = ot[dv:dv + 1, :]
        o_ref[0, r0:r0 + tb, :] = (ot[0:dv, :] / denom).T.astype(o_ref.dtype)
        return jnp.min(denom)

    smallest = block(0, exact=False)
    for i in range(1, n_blocks):
        smallest = jnp.minimum(smallest, block(i, exact=False))

    @pl.when(jnp.logical_not(smallest >= DENOM_FLOOR))
    def _():
        for i in range(n_blocks):
            block(i, exact=True)


def _attn(q, k, v):
    b, nh, s, dq = q.shape
    dv = v.shape[-1]
    tq = min(Q_STEP_BLOCKS * Q_BLOCK, s)
    head = lambda x: pl.BlockSpec((1, 1, s, x), lambda i, h, j: (i, h, 0, 0))
    return pl.pallas_call(
        _attn_kernel,
        out_shape=jax.ShapeDtypeStruct((b, s, nh * dv), BF16),
        grid=(b, nh, s // tq),
        in_specs=[pl.BlockSpec((1, 1, tq, dq), lambda i, h, j: (i, h, j, 0)), head(dq), head(dv)],
        out_specs=pl.BlockSpec((1, tq, dv), lambda i, h, j: (i, j, h)),
        scratch_shapes=[
            pltpu.VMEM((2, s, Q_BLOCK), BF16),
            pltpu.VMEM((dv + BF16_ROWS, s), BF16),
            pltpu.VMEM((1, dq), F32),
        ],
        compiler_params=_params(("parallel", "parallel", "arbitrary")),
        name="attn",
    )(q, k, v)


def _tail_kernel(h_ref, ys_ref, zg_ref, ob_ref, ga_ref, gb_ref, p_ref, wl_ref, wm_ref, wo_ref,
                 g_ref, wg_ref, wu_ref, wd_ref, pg_ref, wpg_ref, wpp_ref, ppg_ref, fg_ref, o_ref):
    ya_in = ys_ref[0].astype(F32) * jax.nn.gelu(zg_ref[0].astype(F32))
    ya = _dot(ya_in.astype(BF16), wl_ref[...])
    yb = _dot(ob_ref[0], wm_ref[...])
    merged = (jax.nn.sigmoid(ga_ref[0].astype(F32)) * ya
              + jax.nn.sigmoid(gb_ref[0].astype(F32)) * yb)
    h = h_ref[0] + _dot(merged.astype(BF16), wo_ref[...])

    h = _swiglu_half(h, g_ref[...], wg_ref, wu_ref, wd_ref)
    gate = jax.nn.sigmoid(_dot(_rmsnorm(h, pg_ref[...]).astype(BF16), wpg_ref[...]))
    emb = _rmsnorm(_dot(p_ref[0].astype(BF16), wpp_ref[...]), ppg_ref[...])
    h = h + gate * emb
    o_ref[0] = _rmsnorm(h, fg_ref[...])


def _tail(h1, ysum, zg, ob, ga, gb, p, weights):
    b, s, d = h1.shape
    t = FFN_ROWS
    row = pl.BlockSpec((1, t, d), lambda i, j: (i, j, 0))
    prow = pl.BlockSpec((None, 1, t, p.shape[-1]), lambda i, j: (0, i, j, 0))
    return pl.pallas_call(
        _tail_kernel,
        out_shape=jax.ShapeDtypeStruct((b, s, d), F32),
        grid=(b, s // t),
        in_specs=[row] * 6 + [prow] + [_resident(a.shape) for a in weights],
        out_specs=row,
        compiler_params=_params(("parallel", "parallel")),
        name="tail",
    )(h1, ysum, zg, ob, ga, gb, p, *weights)


def _swap_halves(w):
    half = w.shape[-1] // 2
    return jnp.concatenate([w[..., half:], w[..., :half]], axis=-1)


def _in_weights(w_in, d, lru_w, q_lora, kv_lora):
    sizes = (lru_w, lru_w, q_lora, kv_lora, QK_ROPE, d, d)
    offs = [0]
    for sz in sizes:
        offs.append(offs[-1] + sz)
    kr = w_in[:, offs[4]:offs[5]]
    pieces = (w_in[:, :offs[4]], jnp.concatenate([kr, _swap_halves(kr)], axis=1),
              w_in[:, offs[5]:offs[6]], w_in[:, offs[6]:offs[7]])
    cols = {name: (0, offs[i], offs[i + 1]) for i, name in enumerate(("zx", "zg", "cq", "ckv"))}
    cols.update(kr=(1, 0, 2 * QK_ROPE), ga=(2, 0, d), gb=(3, 0, d))
    return tuple(piece.astype(BF16) for piece in pieces), cols


def _q_weights(w_uq):
    r = w_uq.shape[0]
    w = w_uq.reshape(r, MLA_HEADS, QK_NOPE + QK_ROPE)
    full = jnp.concatenate([w, _swap_halves(w[:, :, QK_NOPE:])], axis=-1)
    assert full.shape[-1] == QK_PAD
    return full.reshape(r, MLA_HEADS * QK_PAD).astype(BF16)


def kernel(x, p, positions, ffn1_norm, ffn1_w_gate, ffn1_w_up, ffn1_w_down, mix_norm, w_in, conv_w, conv_b, lru_w_r, lru_b_r, lru_w_i, lru_b_i, lru_lambda, w_lru_out, q_norm, w_uq, kv_norm, w_ukv, w_mla_out, w_o, ffn2_norm, ffn2_w_gate, ffn2_w_up, ffn2_w_down, ple_norm, ple_w_gate, ple_w_proj, ple_proj_norm, final_norm):
    b, s, d = x.shape
    depth = ffn1_norm.shape[0]
    lru_w = conv_w.shape[-1]
    q_lora, kv_lora = q_norm.shape[-1], kv_norm.shape[-1]
    assert b == SUBLANES and lru_w == d
    assert depth == 1
    assert s % FFN_ROWS == 0 and s % STEPS == 0 and s % BWD_STEPS == 0
    assert s % Q_BLOCK == 0 and s % min(Q_STEP_BLOCKS * Q_BLOCK, s) == 0

    row = lambda v: v.reshape(1, -1).astype(F32)
    posf = _dense_positions(positions, STEPS)
    half = QK_ROPE // 2
    inv_freq = ROPE_THETA ** (-jnp.arange(0, QK_ROPE, 2, dtype=F32) / QK_ROPE)
    invf = jnp.tile(inv_freq, LANES // half).reshape(1, LANES)
    sign = jnp.tile(jnp.concatenate([-jnp.ones(half, F32), jnp.ones(half, F32)]),
                    LANES // QK_ROPE).reshape(1, LANES)
    q_scale = (QK_NOPE + QK_ROPE) ** -0.5 * math.log2(math.e)

    i = 0
    h1 = _ffn(x, row(ffn1_norm[i]), ffn1_w_gate[i].astype(BF16), ffn1_w_up[i].astype(BF16),
              ffn1_w_down[i].astype(BF16))

    w_parts, cols = _in_weights(w_in[i], d, lru_w, q_lora, kv_lora)
    wqm = _q_weights(w_uq[i])
    lru_args = lambda dr: (
        (0.5 * jnp.concatenate([lru_w_r[i, dr], lru_w_i[i, dr]], axis=-1)).astype(BF16),
        row(0.5 * lru_b_r[i, dr]), row(0.5 * lru_b_i[i, dr]), row(lru_lambda[i, dr]))
    conv = (conv_w[i].astype(F32), row(conv_b[i]))
    zg, ga, gb, q, k, v, hf3, xa3 = _inproj(
        h1, posf, row(mix_norm[i]), w_parts, cols, row(q_norm[i]), wqm, row(kv_norm[i]),
        w_ukv[i].astype(BF16), invf, sign, q_scale, (*conv, *lru_args(0)))
    ysum = _lru_bwd(xa3, hf3, *lru_args(1))

    ob = _attn(q, k, v)

    tail_weights = (
        w_lru_out[i].astype(BF16), w_mla_out[i].astype(BF16), w_o[i].astype(BF16),
        row(ffn2_norm[i]), ffn2_w_gate[i].astype(BF16), ffn2_w_up[i].astype(BF16),
        ffn2_w_down[i].astype(BF16), row(ple_norm[i]), ple_w_gate[i].astype(BF16),
        ple_w_proj[i].astype(BF16), row(ple_proj_norm[i]), row(final_norm))
    return _tail(h1, ysum, zg, ob, ga, gb, p, tail_weights)
```

```python
import functools
import math

import jax
import jax.numpy as jnp
from jax import lax
from jax.experimental import pallas as pl
from jax.experimental.pallas import tpu as pltpu

F32 = jnp.float32
BF16 = jnp.bfloat16

LRU_BLOCKS = 8
CONV_WIDTH = 4
LRU_C = 8.0
MLA_HEADS = 8
QK_NOPE = 128
QK_ROPE = 64
V_HEAD = 128
ROPE_THETA = 10000.0
EPS = 1e-6

LANES = 128
SUBLANES = 8
BF16_ROWS = 16
QK_PAD = 256
STAB_LANE = QK_NOPE + QK_ROPE
DENOM_FLOOR = 2.0 ** -64
VMEM_LIMIT_BYTES = 56 * 1024 * 1024

FFN_ROWS = 512
STEPS = 64
BWD_STEPS = 128
Q_BLOCK = 1024
Q_STEP_BLOCKS = 2
FF_CHUNK = 512


def _resident(shape):
    nd = len(shape)
    return pl.BlockSpec(shape, lambda *_: (0,) * nd, pipeline_mode=pl.Buffered(1))


def _params(semantics):
    return pltpu.CompilerParams(dimension_semantics=semantics,
                                vmem_limit_bytes=VMEM_LIMIT_BYTES)


def _rmsnorm(x, g):
    ms = jnp.mean(x * x, axis=-1, keepdims=True)
    return x * lax.rsqrt(ms + EPS) * g


def _dot(a, b):
    return jnp.dot(a, b, preferred_element_type=F32)


def _dot_nt(a, b):
    return lax.dot_general(a, b, (((1,), (1,)), ((), ())), preferred_element_type=F32)


def _swiglu_half(x, g, wg_ref, wu_ref, wd_ref):
    d_ff = wg_ref.shape[1]
    xn = _rmsnorm(x, g).astype(BF16)
    y = None
    for c0 in range(0, d_ff, FF_CHUNK):
        c1 = min(c0 + FF_CHUNK, d_ff)
        gate = _dot(xn, wg_ref[:, c0:c1])
        up = _dot(xn, wu_ref[:, c0:c1])
        act = (gate * jax.nn.sigmoid(gate) * up).astype(BF16)
        part = _dot(act, wd_ref[c0:c1, :])
        y = part if y is None else y + part
    return x + 0.5 * y


def _ffn_kernel(x_ref, g_ref, wg_ref, wu_ref, wd_ref, o_ref):
    o_ref[0] = _swiglu_half(x_ref[0], g_ref[...], wg_ref, wu_ref, wd_ref)


def _ffn(x, g, wg, wu, wd):
    b, s, d = x.shape
    t = FFN_ROWS
    row = pl.BlockSpec((1, t, d), lambda i, j: (i, j, 0))
    return pl.pallas_call(
        _ffn_kernel,
        out_shape=jax.ShapeDtypeStruct((b, s, d), F32),
        grid=(b, s // t),
        in_specs=[row, _resident(g.shape), _resident(wg.shape), _resident(wu.shape),
                  _resident(wd.shape)],
        out_specs=row,
        compiler_params=_params(("parallel", "parallel")),
        name="ffn",
    )(x, g, wg, wu, wd)


ROPE_GROUPS = LANES // (QK_ROPE // 2)


def _dense_positions(positions, steps):
    b, s = positions.shape
    tiles = s // steps
    rows = b * steps
    per_tile = positions.astype(F32).reshape(b, tiles, steps).transpose(1, 0, 2)
    grouped = per_tile.reshape(tiles, ROPE_GROUPS, rows // ROPE_GROUPS).transpose(0, 2, 1)
    return jnp.repeat(grouped, LANES // ROPE_GROUPS, axis=2)


def _rope_tables(dense_pos, invf, sign):
    groups = ROPE_GROUPS
    q4 = dense_pos.shape[0]
    width = LANES // groups
    lane_grp = lax.broadcasted_iota(jnp.int32, (q4, LANES), 1) // width

    def by_group(vals):
        out = vals[groups - 1]
        for g in range(groups - 2, -1, -1):
            out = jnp.where(lane_grp == g, vals[g], out)
        return out

    ang = dense_pos * invf
    tables = []
    for dense in (jnp.cos(ang), jnp.sin(ang)):
        rolled = [dense] + [pltpu.roll(dense, k * width, axis=1) for k in range(1, groups)]
        tables.append(jnp.concatenate(
            [by_group([rolled[(j - g) % groups] for j in range(groups)]) for g in range(groups)],
            axis=0))
    return tables[0], tables[1] * sign


def _lru_rates(lam):
    nlam = -lam
    softplus = jnp.maximum(nlam, 0.0) + jnp.log1p(jnp.exp(-jnp.abs(nlam)))
    return (-0.5 * LRU_C * math.log2(math.e)) * softplus


def _lru_gate_block(xa, n, rate, wg_ref, br_ref, bi_ref):
    blk = xa.shape[1]
    c0, c1 = n * blk, (n + 1) * blk
    gates = _dot(xa.astype(BF16), wg_ref[n])
    tr = jnp.tanh(gates[:, 0:blk] + br_ref[:, c0:c1])
    ti = jnp.tanh(gates[:, blk:2 * blk] + bi_ref[:, c0:c1])
    a = jnp.exp2(tr * rate[:, c0:c1] + rate[:, c0:c1])
    gap = 1.0 - a * a
    root = jnp.where(gap > 0.0, gap * lax.rsqrt(gap), 0.0)
    xh = 0.5 * xa
    return a, root * (ti * xh + xh)


def _lru_scan(a_s, u_s, h_s, out_ref, steps, reverse):
    def step(j, h):
        t = (steps - 1 - j) if reverse else j
        r0 = pl.multiple_of(t * SUBLANES, SUBLANES)
        h = a_s[:, pl.ds(r0, SUBLANES), :] * h + u_s[:, pl.ds(r0, SUBLANES), :]
        out_ref[:, pl.ds(r0, SUBLANES), :] = h
        return h

    h_s[...] = lax.fori_loop(0, steps, step, h_s[...], unroll=8)


def _inproj_kernel(h_ref, hp_ref, hn_ref, pos_ref, g_ref, w0_ref, w1_ref, w2_ref, w3_ref, qg_ref,
                   wqm_ref, kvg_ref, wkv_ref, invf_ref, sign_ref, cw_ref, cb_ref, wgl_ref,
                   br_ref, bi_ref, lam_ref,
                   zg_ref, ga_ref, gb_ref, q_ref, k_ref, v_ref, hf_ref, xa_ref,
                   xpad, a_s, u_s, h_s, hf_s, *, cols, q_scale):
    i = pl.program_id(0)
    nb, steps, d = h_ref.shape
    rows = nb * steps
    halo = hp_ref.shape[1]
    un = _rmsnorm(h_ref[...].reshape(rows, d), g_ref[...]).astype(BF16)

    def proj(name, lhs=un):
        part, c0, c1 = cols[name]
        return _dot(lhs, (w0_ref, w1_ref, w2_ref, w3_ref)[part][:, c0:c1])

    def per_batch(val):
        return val.reshape(nb, steps, val.shape[-1])

    @pl.when(i == 0)
    def _():
        h_s[...] = jnp.zeros_like(h_s)

    zx = proj("zx")
    h_halo = jnp.concatenate([hp_ref[...].reshape(nb * halo, d), hn_ref[...].reshape(nb * halo, d)])
    zx_halo = proj("zx", lhs=_rmsnorm(h_halo, g_ref[...]).astype(BF16))
    zx_prev = jnp.where(i > 0, zx_halo[0:nb * halo], 0.0)
    zx_next = jnp.where(i < pl.num_programs(0) - 1, zx_halo[nb * halo:], 0.0)
    first = halo * nb
    for n in range(LRU_BLOCKS):
        lanes = slice(n * LANES, (n + 1) * LANES)
        for b in range(nb):
            xpad[n, pl.ds(b, halo, stride=nb), :] = zx_prev[b * halo:(b + 1) * halo, lanes]
            xpad[n, pl.ds(first + b, steps, stride=nb), :] = zx[b * steps:(b + 1) * steps, lanes]
            xpad[n, pl.ds(first + rows + b, halo, stride=nb), :] = (
                zx_next[b * halo:(b + 1) * halo, lanes])

    rate = _lru_rates(lam_ref[...])

    def lru_block(n):
        lanes = slice(n * LANES, (n + 1) * LANES)
        xa = cb_ref[:, lanes]
        for k in range(CONV_WIDTH):
            r0 = first + (k - CONV_WIDTH // 2) * SUBLANES
            xa = xa + xpad[n, r0:r0 + rows, :] * cw_ref[k:k + 1, lanes]
        xa_ref[n] = xa.astype(xa_ref.dtype)
        a_s[n], u_s[n] = _lru_gate_block(xa, n, rate, wgl_ref, br_ref, bi_ref)

    def half_proj(name, out_ref, half):
        part, c0, c1 = cols[name]
        mid = (c0 + c1) // 2
        lo, hi = (c0, mid) if half == 0 else (mid, c1)
        w = (w0_ref, w1_ref, w2_ref, w3_ref)[part]
        out_ref[:, :, lo - c0:hi - c0] = per_batch(_dot(un, w[:, lo:hi]).astype(BF16))

    gate_outputs = [(name, ref, half) for name, ref in (("zg", zg_ref), ("ga", ga_ref),
                                                        ("gb", gb_ref)) for half in (0, 1)]
    for n, gate_output in enumerate(gate_outputs):
        lru_block(n)
        half_proj(*gate_output)
    first_late = len(gate_outputs)
    lru_block(first_late)

    cosv, sinv = _rope_tables(pos_ref[0], invf_ref[...], sign_ref[...])
    lane = lax.broadcasted_iota(jnp.int32, (rows, LANES), 1)
    low = lane < QK_ROPE
    table = jnp.where(low, cosv, sinv)

    def rotate(x, tab):
        y = x * tab
        return jnp.where(low, y + pltpu.roll(y, QK_ROPE, axis=1), 0.0)

    cqn = _rmsnorm(proj("cq"), qg_ref[...]).astype(BF16)
    qm = _dot(cqn, wqm_ref[...])
    q_table = table * q_scale
    for h in range(MLA_HEADS):
        m0 = h * QK_PAD
        q_ref[:, h, :, 0:QK_NOPE] = per_batch((qm[:, m0:m0 + QK_NOPE] * q_scale).astype(BF16))
        rope = rotate(qm[:, m0 + QK_NOPE:m0 + QK_PAD], q_table)
        q_ref[:, h, :, QK_NOPE:QK_PAD] = per_batch(rope.astype(BF16))

    for n in range(first_late + 1, LRU_BLOCKS):
        lru_block(n)

    ckvn =_rmsnorm(proj("ckv"), kvg_ref[...]).astype(BF16)
    kv = _dot(ckvn, wkv_ref[...])
    kpe = per_batch(jnp.where(lane == STAB_LANE - QK_NOPE, 1.0,
                              rotate(proj("kr"), table)).astype(BF16))
    for h in range(MLA_HEADS):
        c0 = h * (QK_NOPE + V_HEAD)
        k_ref[:, h, :, 0:QK_NOPE] = per_batch(kv[:, c0:c0 + QK_NOPE].astype(BF16))
        k_ref[:, h, :, QK_NOPE:QK_PAD] = kpe
        v_ref[:, h, :, :] = per_batch(kv[:, c0 + QK_NOPE:c0 + QK_NOPE + V_HEAD].astype(BF16))

    _lru_scan(a_s, u_s, h_s, hf_s, steps, reverse=False)
    hf_ref[...] = hf_s[...].astype(hf_ref.dtype)


def _inproj(h1, posf, g, w_parts, cols, qg, wqm, kvg, wkv, invf, sign, q_scale, lru_weights):
    b, s, d = h1.shape
    t = STEPS
    halo = SUBLANES
    width = cols["zx"][2] - cols["zx"][1]
    nblk = width // LANES
    rows = t * b
    bm = lambda x: pl.BlockSpec((b, t, x), lambda i: (0, i, 0))
    heads = lambda x: pl.BlockSpec((b, MLA_HEADS, t, x), lambda i: (0, 0, i, 0))
    tm = pl.BlockSpec((nblk, rows, LANES), lambda i: (0, i, 0))
    per_halo = t // halo
    prev = pl.BlockSpec((b, halo, d), lambda i: (0, jnp.maximum(i * per_halo - 1, 0), 0))
    nxt = pl.BlockSpec((b, halo, d), lambda i: (0, jnp.minimum((i + 1) * per_halo,
                                                               s // halo - 1), 0))
    weights = (g, *w_parts, qg, wqm, kvg, wkv, invf, sign, *lru_weights)
    kernel = functools.partial(_inproj_kernel, cols=cols, q_scale=q_scale)
    return pl.pallas_call(
        kernel,
        out_shape=[
            jax.ShapeDtypeStruct((b, s, width), BF16),
            jax.ShapeDtypeStruct((b, s, d), BF16),
            jax.ShapeDtypeStruct((b, s, d), BF16),
            jax.ShapeDtypeStruct((b, MLA_HEADS, s, QK_PAD), BF16),
            jax.ShapeDtypeStruct((b, MLA_HEADS, s, QK_PAD), BF16),
            jax.ShapeDtypeStruct((b, MLA_HEADS, s, V_HEAD), BF16),
            jax.ShapeDtypeStruct((nblk, s * b, LANES), BF16),
            jax.ShapeDtypeStruct((nblk, s * b, LANES), BF16),
        ],
        grid=(s // t,),
        in_specs=[bm(d), prev, nxt, pl.BlockSpec((1,) + posf.shape[1:], lambda i: (i, 0, 0))]
        + [_resident(a.shape) for a in weights],
        out_specs=[bm(width), bm(d), bm(d), heads(QK_PAD), heads(QK_PAD), heads(V_HEAD), tm, tm],
        scratch_shapes=[
            pltpu.VMEM((nblk, rows + 2 * halo * b, LANES), F32),
            pltpu.VMEM((nblk, rows, LANES), F32),
            pltpu.VMEM((nblk, rows, LANES), F32),
            pltpu.VMEM((nblk, SUBLANES, LANES), F32),
            pltpu.VMEM((nblk, rows, LANES), F32),
        ],
        compiler_params=_params(("arbitrary",)),
        name="inproj",
    )(h1, h1, h1, posf, *weights)


def _lru_bwd_kernel(xa_ref, hf_ref, wg_ref, br_ref, bi_ref, lam_ref, o_ref, a_s, u_s, h_s, hb_s,
                    *, steps):
    @pl.when(pl.program_id(0) == 0)
    def _():
        h_s[...] = jnp.zeros_like(h_s)

    rate = _lru_rates(lam_ref[...])
    for n in range(LRU_BLOCKS):
        a_s[n], u_s[n] = _lru_gate_block(xa_ref[n].astype(F32), n, rate, wg_ref, br_ref, bi_ref)
    _lru_scan(a_s, u_s, h_s, hb_s, steps, reverse=True)

    hb_s[...] = hb_s[...] + hf_ref[...].astype(F32)
    blk = xa_ref.shape[2]
    for n in range(LRU_BLOCKS):
        for b in range(SUBLANES):
            o_ref[b, :, n * blk:(n + 1) * blk] = (
                hb_s[n, pl.ds(b, steps, stride=SUBLANES), :].astype(o_ref.dtype))


def _lru_bwd(xa3, hf3, wg, br, bi, lam):
    nblk, n_rows, blk = xa3.shape
    steps = BWD_STEPS
    rows = steps * SUBLANES
    n_tiles = n_rows // rows
    cur = pl.BlockSpec((nblk, rows, blk), lambda i: (0, n_tiles - 1 - i, 0))
    weights = (wg, br, bi, lam)
    return pl.pallas_call(
        functools.partial(_lru_bwd_kernel, steps=steps),
        out_shape=jax.ShapeDtypeStruct((SUBLANES, n_rows // SUBLANES, nblk * blk), BF16),
        grid=(n_tiles,),
        in_specs=[cur, cur] + [_resident(a.shape) for a in weights],
        out_specs=pl.BlockSpec((SUBLANES, steps, nblk * blk), lambda i: (0, n_tiles - 1 - i, 0)),
        scratch_shapes=[
            pltpu.VMEM((nblk, rows, blk), F32),
            pltpu.VMEM((nblk, rows, blk), F32),
            pltpu.VMEM((nblk, SUBLANES, blk), F32),
            pltpu.VMEM((nblk, rows, blk), F32),
        ],
        compiler_params=_params(("arbitrary",)),
        name="lru_bwd",
    )(xa3, hf3, *weights)


def _attn_kernel(q_ref, k_ref, v_ref, o_ref, p_s, vt_s, kmax_s):
    dv = v_ref.shape[-1]
    tb = p_s.shape[-1]
    n_blocks = q_ref.shape[2] // tb
    lane = lax.broadcasted_iota(jnp.int32, (1, q_ref.shape[-1]), 1)

    @pl.when(pl.program_id(2) == 0)
    def _():
        vt_s[0:dv, :] = v_ref[0, 0].astype(F32).T.astype(BF16)
        row = lax.broadcasted_iota(jnp.int32, (BF16_ROWS, vt_s.shape[1]), 0)
        vt_s[dv:dv + BF16_ROWS, :] = jnp.where(row == 0, 1.0, 0.0).astype(BF16)
        kf = k_ref[0, 0].astype(F32)
        ksq = jnp.max(jnp.sum(kf * kf, axis=1, keepdims=True), axis=0, keepdims=True)
        kmax_s[...] = jnp.broadcast_to(jnp.sqrt(ksq), kmax_s.shape)

    def block(i, exact):
        r0 = i * tb
        slot = i % p_s.shape[0]
        qf = q_ref[0, 0, r0:r0 + tb, :].astype(F32)
        bound = jnp.sqrt(jnp.sum(qf * qf, axis=1, keepdims=True)) * kmax_s[...]
        qs = jnp.where(lane == STAB_LANE, -bound, qf).astype(BF16)
        st = _dot_nt(k_ref[0, 0], qs)
        if exact:
            st = st - jnp.max(st, axis=0, keepdims=True)
        p_s[slot] = jnp.exp2(st).astype(BF16)
        ot = _dot(vt_s[...], p_s[slot])
        denom = ot[dv:dv + 1, :]
        o_ref[0, r0:r0 + tb, :] = (ot[0:dv, :] / denom).T.astype(o_ref.dtype)
        return jnp.min(denom)

    smallest = block(0, exact=False)
    for i in range(1, n_blocks):
        smallest = jnp.minimum(smallest, block(i, exact=False))

    @pl.when(jnp.logical_not(smallest >= DENOM_FLOOR))
    def _():
        for i in range(n_blocks):
            block(i, exact=True)


def _attn(q, k, v):
    b, nh, s, dq = q.shape
    dv = v.shape[-1]
    tq = min(Q_STEP_BLOCKS * Q_BLOCK, s)
    head = lambda x: pl.BlockSpec((1, 1, s, x), lambda i, h, j: (i, h, 0, 0))
    return pl.pallas_call(
        _attn_kernel,
        out_shape=jax.ShapeDtypeStruct((b, s, nh * dv), BF16),
        grid=(b, nh, s // tq),
        in_specs=[pl.BlockSpec((1, 1, tq, dq), lambda i, h, j: (i, h, j, 0)), head(dq), head(dv)],
        out_specs=pl.BlockSpec((1, tq, dv), lambda i, h, j: (i, j, h)),
        scratch_shapes=[
            pltpu.VMEM((2, s, Q_BLOCK), BF16),
            pltpu.VMEM((dv + BF16_ROWS, s), BF16),
            pltpu.VMEM((1, dq), F32),
        ],
        compiler_params=_params(("parallel", "parallel", "arbitrary")),
        name="attn",
    )(q, k, v)


def _tail_kernel(h_ref, ys_ref, zg_ref, ob_ref, ga_ref, gb_ref, p_ref, wl_ref, wm_ref, wo_ref,
                 g_ref, wg_ref, wu_ref, wd_ref, pg_ref, wpg_ref, wpp_ref, ppg_ref, fg_ref, o_ref):
    emb = _rmsnorm(_dot(p_ref[0].astype(BF16), wpp_ref[...]), ppg_ref[...])

    ya_in = ys_ref[0].astype(F32) * jax.nn.gelu(zg_ref[0].astype(F32))
    ya = _dot(ya_in.astype(BF16), wl_ref[...])
    yb = _dot(ob_ref[0], wm_ref[...])
    merged = (jax.nn.sigmoid(ga_ref[0].astype(F32)) * ya
              + jax.nn.sigmoid(gb_ref[0].astype(F32)) * yb)
    h = h_ref[0] + _dot(merged.astype(BF16), wo_ref[...])

    h = _swiglu_half(h, g_ref[...], wg_ref, wu_ref, wd_ref)
    gate = jax.nn.sigmoid(_dot(_rmsnorm(h, pg_ref[...]).astype(BF16), wpg_ref[...]))
    h = h + gate * emb
    o_ref[0] = _rmsnorm(h, fg_ref[...])


def _tail(h1, ysum, zg, ob, ga, gb, p, weights):
    b, s, d = h1.shape
    t = FFN_ROWS
    row = pl.BlockSpec((1, t, d), lambda i, j: (i, j, 0))
    prow = pl.BlockSpec((None, 1, t, p.shape[-1]), lambda i, j: (0, i, j, 0))
    return pl.pallas_call(
        _tail_kernel,
        out_shape=jax.ShapeDtypeStruct((b, s, d), F32),
        grid=(b, s // t),
        in_specs=[row] * 6 + [prow] + [_resident(a.shape) for a in weights],
        out_specs=row,
        compiler_params=_params(("parallel", "parallel")),
        name="tail",
    )(h1, ysum, zg, ob, ga, gb, p, *weights)


def _swap_halves(w):
    half = w.shape[-1] // 2
    return jnp.concatenate([w[..., half:], w[..., :half]], axis=-1)


def _in_weights(w_in, d, lru_w, q_lora, kv_lora):
    sizes = (lru_w, lru_w, q_lora, kv_lora, QK_ROPE, d, d)
    offs = [0]
    for sz in sizes:
        offs.append(offs[-1] + sz)
    kr = w_in[:, offs[4]:offs[5]]
    pieces = (w_in[:, :offs[4]], jnp.concatenate([kr, _swap_halves(kr)], axis=1),
              w_in[:, offs[5]:offs[6]], w_in[:, offs[6]:offs[7]])
    cols = {name: (0, offs[i], offs[i + 1]) for i, name in enumerate(("zx", "zg", "cq", "ckv"))}
    cols.update(kr=(1, 0, 2 * QK_ROPE), ga=(2, 0, d), gb=(3, 0, d))
    return tuple(piece.astype(BF16) for piece in pieces), cols


def _q_weights(w_uq):
    r = w_uq.shape[0]
    w = w_uq.reshape(r, MLA_HEADS, QK_NOPE + QK_ROPE)
    full = jnp.concatenate([w, _swap_halves(w[:, :, QK_NOPE:])], axis=-1)
    assert full.shape[-1] == QK_PAD
    return full.reshape(r, MLA_HEADS * QK_PAD).astype(BF16)


def kernel(x, p, positions, ffn1_norm, ffn1_w_gate, ffn1_w_up, ffn1_w_down, mix_norm, w_in, conv_w, conv_b, lru_w_r, lru_b_r, lru_w_i, lru_b_i, lru_lambda, w_lru_out, q_norm, w_uq, kv_norm, w_ukv, w_mla_out, w_o, ffn2_norm, ffn2_w_gate, ffn2_w_up, ffn2_w_down, ple_norm, ple_w_gate, ple_w_proj, ple_proj_norm, final_norm):
    b, s, d = x.shape
    depth = ffn1_norm.shape[0]
    lru_w = conv_w.shape[-1]
    q_lora, kv_lora = q_norm.shape[-1], kv_norm.shape[-1]
    assert b == SUBLANES and lru_w == d
    assert depth == 1
    assert s % FFN_ROWS == 0 and s % STEPS == 0 and s % BWD_STEPS == 0
    assert s % Q_BLOCK == 0 and s % min(Q_STEP_BLOCKS * Q_BLOCK, s) == 0

    row = lambda v: v.reshape(1, -1).astype(F32)
    posf = _dense_positions(positions, STEPS)
    half = QK_ROPE // 2
    inv_freq = ROPE_THETA ** (-jnp.arange(0, QK_ROPE, 2, dtype=F32) / QK_ROPE)
    invf = jnp.tile(inv_freq, LANES // half).reshape(1, LANES)
    sign = jnp.tile(jnp.concatenate([-jnp.ones(half, F32), jnp.ones(half, F32)]),
                    LANES // QK_ROPE).reshape(1, LANES)
    q_scale = (QK_NOPE + QK_ROPE) ** -0.5 * math.log2(math.e)

    i = 0
    h1 = _ffn(x, row(ffn1_norm[i]), ffn1_w_gate[i].astype(BF16), ffn1_w_up[i].astype(BF16),
              ffn1_w_down[i].astype(BF16))

    w_parts, cols = _in_weights(w_in[i], d, lru_w, q_lora, kv_lora)
    wqm = _q_weights(w_uq[i])
    lru_args = lambda dr: (
        (0.5 * jnp.concatenate([lru_w_r[i, dr], lru_w_i[i, dr]], axis=-1)).astype(BF16),
        row(0.5 * lru_b_r[i, dr]), row(0.5 * lru_b_i[i, dr]), row(lru_lambda[i, dr]))
    conv = (conv_w[i].astype(F32), row(conv_b[i]))
    zg, ga, gb, q, k, v, hf3, xa3 = _inproj(
        h1, posf, row(mix_norm[i]), w_parts, cols, row(q_norm[i]), wqm, row(kv_norm[i]),
        w_ukv[i].astype(BF16), invf, sign, q_scale, (*conv, *lru_args(0)))
    ysum = _lru_bwd(xa3, hf3, *lru_args(1))

    ob = _attn(q, k, v)

    tail_weights = (
        w_lru_out[i].astype(BF16), w_mla_out[i].astype(BF16), w_o[i].astype(BF16),
        row(ffn2_norm[i]), ffn2_w_gate[i].astype(BF16), ffn2_w_up[i].astype(BF16),
        ffn2_w_down[i].astype(BF16), row(ple_norm[i]), ple_w_gate[i].astype(BF16),
        ple_w_proj[i].astype(BF16), row(ple_proj_norm[i]), row(final_norm))
    return _tail(h1, ysum, zg, ob, ga, gb, p, tail_weights)
```

```python
import functools
import math

import jax
import jax.numpy as jnp
from jax import lax
from jax.experimental import pallas as pl
from jax.experimental.pallas import tpu as pltpu

F32 = jnp.float32
BF16 = jnp.bfloat16

LRU_BLOCKS = 8
CONV_WIDTH = 4
LRU_C = 8.0
MLA_HEADS = 8
QK_NOPE = 128
QK_ROPE = 64
V_HEAD = 128
ROPE_THETA = 10000.0
EPS = 1e-6

LANES = 128
SUBLANES = 8
QK_PAD = 256
STAB_LANE = QK_NOPE + QK_ROPE
DENOM_FLOOR = 2.0 ** -64
VMEM_LIMIT_BYTES = 56 * 1024 * 1024

FFN_ROWS = 512
STEPS = 64
BWD_STEPS = 128
Q_BLOCK = 1024
Q_STEP_BLOCKS = 2
FF_CHUNK = 512


def _resident(shape):
    nd = len(shape)
    return pl.BlockSpec(shape, lambda *_: (0,) * nd, pipeline_mode=pl.Buffered(1))


def _params(semantics):
    return pltpu.CompilerParams(dimension_semantics=semantics,
                                vmem_limit_bytes=VMEM_LIMIT_BYTES)


def _rmsnorm(x, g):
    ms = jnp.mean(x * x, axis=-1, keepdims=True)
    return x * lax.rsqrt(ms + EPS) * g


def _dot(a, b):
    return jnp.dot(a, b, preferred_element_type=F32)


def _dot_nt(a, b):
    return lax.dot_general(a, b, (((1,), (1,)), ((), ())), preferred_element_type=F32)


def _swiglu_half(x, g, wg_ref, wu_ref, wd_ref):
    d_ff = wg_ref.shape[1]
    xn = _rmsnorm(x, g).astype(BF16)
    y = None
    for c0 in range(0, d_ff, FF_CHUNK):
        c1 = min(c0 + FF_CHUNK, d_ff)
        gate = _dot(xn, wg_ref[:, c0:c1])
        up = _dot(xn, wu_ref[:, c0:c1])
        act = (gate * jax.nn.sigmoid(gate) * up).astype(BF16)
        part = _dot(act, wd_ref[c0:c1, :])
        y = part if y is None else y + part
    return x + 0.5 * y


def _ffn_kernel(x_ref, g_ref, wg_ref, wu_ref, wd_ref, o_ref):
    o_ref[0] = _swiglu_half(x_ref[0], g_ref[...], wg_ref, wu_ref, wd_ref)


def _ffn(x, g, wg, wu, wd):
    b, s, d = x.shape
    t = FFN_ROWS
    row = pl.BlockSpec((1, t, d), lambda i, j: (i, j, 0))
    return pl.pallas_call(
        _ffn_kernel,
        out_shape=jax.ShapeDtypeStruct((b, s, d), F32),
        grid=(b, s // t),
        in_specs=[row, _resident(g.shape), _resident(wg.shape), _resident(wu.shape),
                  _resident(wd.shape)],
        out_specs=row,
        compiler_params=_params(("parallel", "parallel")),
        name="ffn",
    )(x, g, wg, wu, wd)


ROPE_GROUPS = LANES // (QK_ROPE // 2)


def _dense_positions(positions, steps):
    b, s = positions.shape
    tiles = s // steps
    rows = b * steps
    per_tile = positions.astype(F32).reshape(b, tiles, steps).transpose(1, 0, 2)
    grouped = per_tile.reshape(tiles, ROPE_GROUPS, rows // ROPE_GROUPS).transpose(0, 2, 1)
    return jnp.repeat(grouped, LANES // ROPE_GROUPS, axis=2)


def _rope_tables(dense_pos, invf, sign):
    groups = ROPE_GROUPS
    q4 = dense_pos.shape[0]
    width = LANES // groups
    lane_grp = lax.broadcasted_iota(jnp.int32, (q4, LANES), 1) // width

    def by_group(vals):
        out = vals[groups - 1]
        for g in range(groups - 2, -1, -1):
            out = jnp.where(lane_grp == g, vals[g], out)
        return out

    ang = dense_pos * invf
    tables = []
    for dense in (jnp.cos(ang), jnp.sin(ang)):
        rolled = [dense] + [pltpu.roll(dense, k * width, axis=1) for k in range(1, groups)]
        tables.append(jnp.concatenate(
            [by_group([rolled[(j - g) % groups] for j in range(groups)]) for g in range(groups)],
            axis=0))
    return tables[0], tables[1] * sign


def _lru_rates(lam):
    nlam = -lam
    softplus = jnp.maximum(nlam, 0.0) + jnp.log1p(jnp.exp(-jnp.abs(nlam)))
    return (-0.5 * LRU_C * math.log2(math.e)) * softplus


def _lru_gate_block(xa, n, rate, wg_ref, br_ref, bi_ref):
    blk = xa.shape[1]
    c0, c1 = n * blk, (n + 1) * blk
    gates = _dot(xa.astype(BF16), wg_ref[n])
    tr = jnp.tanh(gates[:, 0:blk] + br_ref[:, c0:c1])
    ti = jnp.tanh(gates[:, blk:2 * blk] + bi_ref[:, c0:c1])
    a = jnp.exp2(tr * rate[:, c0:c1] + rate[:, c0:c1])
    gap = 1.0 - a * a
    root = jnp.where(gap > 0.0, gap * lax.rsqrt(gap), 0.0)
    xh = 0.5 * xa
    return a, root * (ti * xh + xh)


def _lru_scan(a_s, u_s, h_s, out_ref, steps, reverse):
    def step(j, h):
        t = (steps - 1 - j) if reverse else j
        r0 = pl.multiple_of(t * SUBLANES, SUBLANES)
        h = a_s[:, pl.ds(r0, SUBLANES), :] * h + u_s[:, pl.ds(r0, SUBLANES), :]
        out_ref[:, pl.ds(r0, SUBLANES), :] = h
        return h

    h_s[...] = lax.fori_loop(0, steps, step, h_s[...], unroll=8)


def _inproj_kernel(h_ref, hp_ref, hn_ref, pos_ref, g_ref, w0_ref, w1_ref, w2_ref, w3_ref, qg_ref,
                   wqm_ref, kvg_ref, wkv_ref, invf_ref, sign_ref, cw_ref, cb_ref, wgl_ref,
                   br_ref, bi_ref, lam_ref,
                   zg_ref, ga_ref, gb_ref, q_ref, k_ref, v_ref, hf_ref, xa_ref,
                   xpad, a_s, u_s, h_s, hf_s, *, cols, q_scale):
    i = pl.program_id(0)
    nb, steps, d = h_ref.shape
    rows = nb * steps
    halo = hp_ref.shape[1]
    un = _rmsnorm(h_ref[...].reshape(rows, d), g_ref[...]).astype(BF16)

    def proj(name, lhs=un):
        part, c0, c1 = cols[name]
        return _dot(lhs, (w0_ref, w1_ref, w2_ref, w3_ref)[part][:, c0:c1])

    def per_batch(val):
        return val.reshape(nb, steps, val.shape[-1])

    @pl.when(i == 0)
    def _():
        h_s[...] = jnp.zeros_like(h_s)

    zx = proj("zx")
    h_halo = jnp.concatenate([hp_ref[...].reshape(nb * halo, d), hn_ref[...].reshape(nb * halo, d)])
    zx_halo = proj("zx", lhs=_rmsnorm(h_halo, g_ref[...]).astype(BF16))
    zx_prev = jnp.where(i > 0, zx_halo[0:nb * halo], 0.0)
    zx_next = jnp.where(i < pl.num_programs(0) - 1, zx_halo[nb * halo:], 0.0)
    first = halo * nb
    for n in range(LRU_BLOCKS):
        lanes = slice(n * LANES, (n + 1) * LANES)
        for b in range(nb):
            xpad[n, pl.ds(b, halo, stride=nb), :] = zx_prev[b * halo:(b + 1) * halo, lanes]
            xpad[n, pl.ds(first + b, steps, stride=nb), :] = zx[b * steps:(b + 1) * steps, lanes]
            xpad[n, pl.ds(first + rows + b, halo, stride=nb), :] = (
                zx_next[b * halo:(b + 1) * halo, lanes])

    rate = _lru_rates(lam_ref[...])

    def lru_block(n):
        lanes = slice(n * LANES, (n + 1) * LANES)
        xa = cb_ref[:, lanes]
        for k in range(CONV_WIDTH):
            r0 = first + (k - CONV_WIDTH // 2) * SUBLANES
            xa = xa + xpad[n, r0:r0 + rows, :] * cw_ref[k:k + 1, lanes]
        xa_ref[n] = xa.astype(xa_ref.dtype)
        a_s[n], u_s[n] = _lru_gate_block(xa, n, rate, wgl_ref, br_ref, bi_ref)

    def half_proj(name, out_ref, half):
        part, c0, c1 = cols[name]
        mid = (c0 + c1) // 2
        lo, hi = (c0, mid) if half == 0 else (mid, c1)
        w = (w0_ref, w1_ref, w2_ref, w3_ref)[part]
        out_ref[:, :, lo - c0:hi - c0] = per_batch(_dot(un, w[:, lo:hi]).astype(BF16))

    gate_outputs = [(name, ref, half) for name, ref in (("zg", zg_ref), ("ga", ga_ref),
                                                        ("gb", gb_ref)) for half in (0, 1)]
    for n, gate_output in enumerate(gate_outputs):
        lru_block(n)
        half_proj(*gate_output)
    first_late = len(gate_outputs)
    lru_block(first_late)

    cosv, sinv = _rope_tables(pos_ref[0], invf_ref[...], sign_ref[...])
    lane = lax.broadcasted_iota(jnp.int32, (rows, LANES), 1)
    low = lane < QK_ROPE
    table = jnp.where(low, cosv, sinv)

    def rotate(x, tab):
        y = x * tab
        return jnp.where(low, y + pltpu.roll(y, QK_ROPE, axis=1), 0.0)

    cqn = _rmsnorm(proj("cq"), qg_ref[...]).astype(BF16)
    qm = _dot(cqn, wqm_ref[...])
    q_table = table * q_scale
    for h in range(MLA_HEADS):
        m0 = h * QK_PAD
        q_ref[:, h, :, 0:QK_NOPE] = per_batch((qm[:, m0:m0 + QK_NOPE] * q_scale).astype(BF16))
        rope = rotate(qm[:, m0 + QK_NOPE:m0 + QK_PAD], q_table)
        q_ref[:, h, :, QK_NOPE:QK_PAD] = per_batch(rope.astype(BF16))

    for n in range(first_late + 1, LRU_BLOCKS):
        lru_block(n)

    ckvn =_rmsnorm(proj("ckv"), kvg_ref[...]).astype(BF16)
    kv = _dot(ckvn, wkv_ref[...])
    kpe = per_batch(jnp.where(lane == STAB_LANE - QK_NOPE, 1.0,
                              rotate(proj("kr"), table)).astype(BF16))
    for h in range(MLA_HEADS):
        c0 = h * (QK_NOPE + V_HEAD)
        k_ref[:, h, :, 0:QK_NOPE] = per_batch(kv[:, c0:c0 + QK_NOPE].astype(BF16))
        k_ref[:, h, :, QK_NOPE:QK_PAD] = kpe
        v_ref[:, h, :, :] = per_batch(kv[:, c0 + QK_NOPE:c0 + QK_NOPE + V_HEAD].astype(BF16))

    _lru_scan(a_s, u_s, h_s, hf_s, steps, reverse=False)
    hf_ref[...] = hf_s[...].astype(hf_ref.dtype)


def _inproj(h1, posf, g, w_parts, cols, qg, wqm, kvg, wkv, invf, sign, q_scale, lru_weights):
    b, s, d = h1.shape
    t = STEPS
    halo = SUBLANES
    width = cols["zx"][2] - cols["zx"][1]
    nblk = width // LANES
    rows = t * b
    bm = lambda x: pl.BlockSpec((b, t, x), lambda i: (0, i, 0))
    heads = lambda x: pl.BlockSpec((b, MLA_HEADS, t, x), lambda i: (0, 0, i, 0))
    tm = pl.BlockSpec((nblk, rows, LANES), lambda i: (0, i, 0))
    per_halo = t // halo
    prev = pl.BlockSpec((b, halo, d), lambda i: (0, jnp.maximum(i * per_halo - 1, 0), 0))
    nxt = pl.BlockSpec((b, halo, d), lambda i: (0, jnp.minimum((i + 1) * per_halo,
                                                               s // halo - 1), 0))
    weights = (g, *w_parts, qg, wqm, kvg, wkv, invf, sign, *lru_weights)
    kernel = functools.partial(_inproj_kernel, cols=cols, q_scale=q_scale)
    return pl.pallas_call(
        kernel,
        out_shape=[
            jax.ShapeDtypeStruct((b, s, width), BF16),
            jax.ShapeDtypeStruct((b, s, d), BF16),
            jax.ShapeDtypeStruct((b, s, d), BF16),
            jax.ShapeDtypeStruct((b, MLA_HEADS, s, QK_PAD), BF16),
            jax.ShapeDtypeStruct((b, MLA_HEADS, s, QK_PAD), BF16),
            jax.ShapeDtypeStruct((b, MLA_HEADS, s, V_HEAD), BF16),
            jax.ShapeDtypeStruct((nblk, s * b, LANES), BF16),
            jax.ShapeDtypeStruct((nblk, s * b, LANES), BF16),
        ],
        grid=(s // t,),
        in_specs=[bm(d), prev, nxt, pl.BlockSpec((1,) + posf.shape[1:], lambda i: (i, 0, 0))]
        + [_resident(a.shape) for a in weights],
        out_specs=[bm(width), bm(d), bm(d), heads(QK_PAD), heads(QK_PAD), heads(V_HEAD), tm, tm],
        scratch_shapes=[
            pltpu.VMEM((nblk, rows + 2 * halo * b, LANES), F32),
            pltpu.VMEM((nblk, rows, LANES), F32),
            pltpu.VMEM((nblk, rows, LANES), F32),
            pltpu.VMEM((nblk, SUBLANES, LANES), F32),
            pltpu.VMEM((nblk, rows, LANES), F32),
        ],
        compiler_params=_params(("arbitrary",)),
        name="inproj",
    )(h1, h1, h1, posf, *weights)


def _lru_bwd_kernel(xa_ref, hf_ref, wg_ref, br_ref, bi_ref, lam_ref, o_ref, a_s, u_s, h_s, hb_s,
                    *, steps):
    @pl.when(pl.program_id(0) == 0)
    def _():
        h_s[...] = jnp.zeros_like(h_s)

    rate = _lru_rates(lam_ref[...])
    for n in range(LRU_BLOCKS):
        a_s[n], u_s[n] = _lru_gate_block(xa_ref[n].astype(F32), n, rate, wg_ref, br_ref, bi_ref)
    _lru_scan(a_s, u_s, h_s, hb_s, steps, reverse=True)

    hb_s[...] = hb_s[...] + hf_ref[...].astype(F32)
    blk = xa_ref.shape[2]
    for n in range(LRU_BLOCKS):
        for b in range(SUBLANES):
            o_ref[b, :, n * blk:(n + 1) * blk] = (
                hb_s[n, pl.ds(b, steps, stride=SUBLANES), :].astype(o_ref.dtype))


def _lru_bwd(xa3, hf3, wg, br, bi, lam):
    nblk, n_rows, blk = xa3.shape
    steps = BWD_STEPS
    rows = steps * SUBLANES
    n_tiles = n_rows // rows
    cur = pl.BlockSpec((nblk, rows, blk), lambda i: (0, n_tiles - 1 - i, 0))
    weights = (wg, br, bi, lam)
    return pl.pallas_call(
        functools.partial(_lru_bwd_kernel, steps=steps),
        out_shape=jax.ShapeDtypeStruct((SUBLANES, n_rows // SUBLANES, nblk * blk), BF16),
        grid=(n_tiles,),
        in_specs=[cur, cur] + [_resident(a.shape) for a in weights],
        out_specs=pl.BlockSpec((SUBLANES, steps, nblk * blk), lambda i: (0, n_tiles - 1 - i, 0)),
        scratch_shapes=[
            pltpu.VMEM((nblk, rows, blk), F32),
            pltpu.VMEM((nblk, rows, blk), F32),
            pltpu.VMEM((nblk, SUBLANES, blk), F32),
            pltpu.VMEM((nblk, rows, blk), F32),
        ],
        compiler_params=_params(("arbitrary",)),
        name="lru_bwd",
    )(xa3, hf3, *weights)


def _attn_kernel(q_ref, k_ref, v_ref, o_ref, p_s, vt_s, kmax_s):
    dv = v_ref.shape[-1]
    tb = p_s.shape[-1]
    n_blocks = q_ref.shape[2] // tb
    lane = lax.broadcasted_iota(jnp.int32, (1, q_ref.shape[-1]), 1)

    @pl.when(pl.program_id(2) == 0)
    def _():
        vt_s[...] = v_ref[0, 0].astype(F32).T.astype(BF16)
        kf = k_ref[0, 0].astype(F32)
        ksq = jnp.max(jnp.sum(kf * kf, axis=1, keepdims=True), axis=0, keepdims=True)
        kmax_s[...] = jnp.broadcast_to(jnp.sqrt(ksq), kmax_s.shape)

    def block(i, exact):
        r0 = i * tb
        slot = i % p_s.shape[0]
        qf = q_ref[0, 0, r0:r0 + tb, :].astype(F32)
        bound = jnp.sqrt(jnp.sum(qf * qf, axis=1, keepdims=True)) * kmax_s[...]
        qs = jnp.where(lane == STAB_LANE, -bound, qf).astype(BF16)
        st = _dot_nt(k_ref[0, 0], qs)
        if exact:
            st = st - jnp.max(st, axis=0, keepdims=True)
        p = jnp.exp2(st)
        p_s[slot] = p.astype(BF16)
        denom = jnp.sum(p, axis=0, keepdims=True)
        ot = _dot(vt_s[...], p_s[slot])
        o_ref[0, r0:r0 + tb, :] = (ot / denom).T.astype(o_ref.dtype)
        return jnp.min(denom)

    smallest = block(0, exact=False)
    for i in range(1, n_blocks):
        smallest = jnp.minimum(smallest, block(i, exact=False))

    @pl.when(jnp.logical_not(smallest >= DENOM_FLOOR))
    def _():
        for i in range(n_blocks):
            block(i, exact=True)


def _attn(q, k, v):
    b, nh, s, dq = q.shape
    dv = v.shape[-1]
    tq = min(Q_STEP_BLOCKS * Q_BLOCK, s)
    head = lambda x: pl.BlockSpec((1, 1, s, x), lambda i, h, j: (i, h, 0, 0))
    return pl.pallas_call(
        _attn_kernel,
        out_shape=jax.ShapeDtypeStruct((b, s, nh * dv), BF16),
        grid=(b, nh, s // tq),
        in_specs=[pl.BlockSpec((1, 1, tq, dq), lambda i, h, j: (i, h, j, 0)), head(dq), head(dv)],
        out_specs=pl.BlockSpec((1, tq, dv), lambda i, h, j: (i, j, h)),
        scratch_shapes=[
            pltpu.VMEM((2, s, Q_BLOCK), BF16),
            pltpu.VMEM((dv, s), BF16),
            pltpu.VMEM((1, dq), F32),
        ],
        compiler_params=_params(("parallel", "parallel", "arbitrary")),
        name="attn",
    )(q, k, v)


def _tail_kernel(h_ref, ys_ref, zg_ref, ob_ref, ga_ref, gb_ref, p_ref, wl_ref, wm_ref, wo_ref,
                 g_ref, wg_ref, wu_ref, wd_ref, pg_ref, wpg_ref, wpp_ref, ppg_ref, fg_ref, o_ref):
    emb = _rmsnorm(_dot(p_ref[0].astype(BF16), wpp_ref[...]), ppg_ref[...])

    ya_in = ys_ref[0].astype(F32) * jax.nn.gelu(zg_ref[0].astype(F32))
    ya = _dot(ya_in.astype(BF16), wl_ref[...])
    yb = _dot(ob_ref[0], wm_ref[...])
    merged = (jax.nn.sigmoid(ga_ref[0].astype(F32)) * ya
              + jax.nn.sigmoid(gb_ref[0].astype(F32)) * yb)
    h = h_ref[0] + _dot(merged.astype(BF16), wo_ref[...])

    h = _swiglu_half(h, g_ref[...], wg_ref, wu_ref, wd_ref)
    gate = jax.nn.sigmoid(_dot(_rmsnorm(h, pg_ref[...]).astype(BF16), wpg_ref[...]))
    h = h + gate * emb
    o_ref[0] = _rmsnorm(h, fg_ref[...])


def _tail(h1, ysum, zg, ob, ga, gb, p, weights):
    b, s, d = h1.shape
    t = FFN_ROWS
    row = pl.BlockSpec((1, t, d), lambda i, j: (i, j, 0))
    prow = pl.BlockSpec((None, 1, t, p.shape[-1]), lambda i, j: (0, i, j, 0))
    return pl.pallas_call(
        _tail_kernel,
        out_shape=jax.ShapeDtypeStruct((b, s, d), F32),
        grid=(b, s // t),
        in_specs=[row] * 6 + [prow] + [_resident(a.shape) for a in weights],
        out_specs=row,
        compiler_params=_params(("parallel", "parallel")),
        name="tail",
    )(h1, ysum, zg, ob, ga, gb, p, *weights)


def _swap_halves(w):
    half = w.shape[-1] // 2
    return jnp.concatenate([w[..., half:], w[..., :half]], axis=-1)


def _in_weights(w_in, d, lru_w, q_lora, kv_lora):
    sizes = (lru_w, lru_w, q_lora, kv_lora, QK_ROPE, d, d)
    offs = [0]
    for sz in sizes:
        offs.append(offs[-1] + sz)
    kr = w_in[:, offs[4]:offs[5]]
    pieces = (w_in[:, :offs[4]], jnp.concatenate([kr, _swap_halves(kr)], axis=1),
              w_in[:, offs[5]:offs[6]], w_in[:, offs[6]:offs[7]])
    cols = {name: (0, offs[i], offs[i + 1]) for i, name in enumerate(("zx", "zg", "cq", "ckv"))}
    cols.update(kr=(1, 0, 2 * QK_ROPE), ga=(2, 0, d), gb=(3, 0, d))
    return tuple(piece.astype(BF16) for piece in pieces), cols


def _q_weights(w_uq):
    r = w_uq.shape[0]
    w = w_uq.reshape(r, MLA_HEADS, QK_NOPE + QK_ROPE)
    full = jnp.concatenate([w, _swap_halves(w[:, :, QK_NOPE:])], axis=-1)
    assert full.shape[-1] == QK_PAD
    return full.reshape(r, MLA_HEADS * QK_PAD).astype(BF16)


def kernel(x, p, positions, ffn1_norm, ffn1_w_gate, ffn1_w_up, ffn1_w_down, mix_norm, w_in, conv_w, conv_b, lru_w_r, lru_b_r, lru_w_i, lru_b_i, lru_lambda, w_lru_out, q_norm, w_uq, kv_norm, w_ukv, w_mla_out, w_o, ffn2_norm, ffn2_w_gate, ffn2_w_up, ffn2_w_down, ple_norm, ple_w_gate, ple_w_proj, ple_proj_norm, final_norm):
    b, s, d = x.shape
    depth = ffn1_norm.shape[0]
    lru_w = conv_w.shape[-1]
    q_lora, kv_lora = q_norm.shape[-1], kv_norm.shape[-1]
    assert b == SUBLANES and lru_w == d
    assert depth == 1
    assert s % FFN_ROWS == 0 and s % STEPS == 0 and s % BWD_STEPS == 0
    assert s % Q_BLOCK == 0 and s % min(Q_STEP_BLOCKS * Q_BLOCK, s) == 0

    row = lambda v: v.reshape(1, -1).astype(F32)
    posf = _dense_positions(positions, STEPS)
    half = QK_ROPE // 2
    inv_freq = ROPE_THETA ** (-jnp.arange(0, QK_ROPE, 2, dtype=F32) / QK_ROPE)
    invf = jnp.tile(inv_freq, LANES // half).reshape(1, LANES)
    sign = jnp.tile(jnp.concatenate([-jnp.ones(half, F32), jnp.ones(half, F32)]),
                    LANES // QK_ROPE).reshape(1, LANES)
    q_scale = (QK_NOPE + QK_ROPE) ** -0.5 * math.log2(math.e)

    i = 0
    h1 = _ffn(x, row(ffn1_norm[i]), ffn1_w_gate[i].astype(BF16), ffn1_w_up[i].astype(BF16),
              ffn1_w_down[i].astype(BF16))

    w_parts, cols = _in_weights(w_in[i], d, lru_w, q_lora, kv_lora)
    wqm = _q_weights(w_uq[i])
    lru_args = lambda dr: (
        (0.5 * jnp.concatenate([lru_w_r[i, dr], lru_w_i[i, dr]], axis=-1)).astype(BF16),
        row(0.5 * lru_b_r[i, dr]), row(0.5 * lru_b_i[i, dr]), row(lru_lambda[i, dr]))
    conv = (conv_w[i].astype(F32), row(conv_b[i]))
    zg, ga, gb, q, k, v, hf3, xa3 = _inproj(
        h1, posf, row(mix_norm[i]), w_parts, cols, row(q_norm[i]), wqm, row(kv_norm[i]),
        w_ukv[i].astype(BF16), invf, sign, q_scale, (*conv, *lru_args(0)))
    ysum = _lru_bwd(xa3, hf3, *lru_args(1))

    ob = _attn(q, k, v)

    tail_weights = (
        w_lru_out[i].astype(BF16), w_mla_out[i].astype(BF16), w_o[i].astype(BF16),
        row(ffn2_norm[i]), ffn2_w_gate[i].astype(BF16), ffn2_w_up[i].astype(BF16),
        ffn2_w_down[i].astype(BF16), row(ple_norm[i]), ple_w_gate[i].astype(BF16),
        ple_w_proj[i].astype(BF16), row(ple_proj_norm[i]), row(final_norm))
    return _tail(h1, ysum, zg, ob, ga, gb, p, tail_weights)
```

```python
import functools
import math

import jax
import jax.numpy as jnp
from jax import lax
from jax.experimental import pallas as pl
from jax.experimental.pallas import tpu as pltpu

F32 = jnp.float32
BF16 = jnp.bfloat16

LRU_BLOCKS = 8
CONV_WIDTH = 4
LRU_C = 8.0
MLA_HEADS = 8
QK_NOPE = 128
QK_ROPE = 64
V_HEAD = 128
ROPE_THETA = 10000.0
EPS = 1e-6

LANES = 128
SUBLANES = 8
QK_PAD = 256
STAB_LANE = QK_NOPE + QK_ROPE
DENOM_FLOOR = 2.0 ** -64
VMEM_LIMIT_BYTES = 56 * 1024 * 1024

FFN_ROWS = 512
STEPS = 64
BWD_STEPS = 128
Q_BLOCK = 1024
Q_STEP_BLOCKS = 2
FF_CHUNK = 512


def _resident(shape):
    nd = len(shape)
    return pl.BlockSpec(shape, lambda *_: (0,) * nd, pipeline_mode=pl.Buffered(1))


def _params(semantics):
    return pltpu.CompilerParams(dimension_semantics=semantics,
                                vmem_limit_bytes=VMEM_LIMIT_BYTES)


def _rmsnorm(x, g):
    ms = jnp.mean(x * x, axis=-1, keepdims=True)
    return x * lax.rsqrt(ms + EPS) * g


def _dot(a, b):
    return jnp.dot(a, b, preferred_element_type=F32)


def _dot_nt(a, b):
    return lax.dot_general(a, b, (((1,), (1,)), ((), ())), preferred_element_type=F32)


def _swiglu_half(x, g, wg_ref, wu_ref, wd_ref):
    d_ff = wg_ref.shape[1]
    xn = _rmsnorm(x, g).astype(BF16)
    y = None
    for c0 in range(0, d_ff, FF_CHUNK):
        c1 = min(c0 + FF_CHUNK, d_ff)
        gate = _dot(xn, wg_ref[:, c0:c1])
        up = _dot(xn, wu_ref[:, c0:c1])
        act = (gate * jax.nn.sigmoid(gate) * up).astype(BF16)
        part = _dot(act, wd_ref[c0:c1, :])
        y = part if y is None else y + part
    return x + 0.5 * y


def _ffn_kernel(x_ref, g_ref, wg_ref, wu_ref, wd_ref, o_ref):
    o_ref[0] = _swiglu_half(x_ref[0], g_ref[...], wg_ref, wu_ref, wd_ref)


def _ffn(x, g, wg, wu, wd):
    b, s, d = x.shape
    t = FFN_ROWS
    row = pl.BlockSpec((1, t, d), lambda i, j: (i, j, 0))
    return pl.pallas_call(
        _ffn_kernel,
        out_shape=jax.ShapeDtypeStruct((b, s, d), F32),
        grid=(b, s // t),
        in_specs=[row, _resident(g.shape), _resident(wg.shape), _resident(wu.shape),
                  _resident(wd.shape)],
        out_specs=row,
        compiler_params=_params(("parallel", "parallel")),
        name="ffn",
    )(x, g, wg, wu, wd)


ROPE_GROUPS = LANES // (QK_ROPE // 2)


def _dense_positions(positions, steps):
    b, s = positions.shape
    tiles = s // steps
    rows = b * steps
    per_tile = positions.astype(F32).reshape(b, tiles, steps).transpose(1, 0, 2)
    grouped = per_tile.reshape(tiles, ROPE_GROUPS, rows // ROPE_GROUPS).transpose(0, 2, 1)
    return jnp.repeat(grouped, LANES // ROPE_GROUPS, axis=2)


def _rope_tables(dense_pos, invf, sign):
    groups = ROPE_GROUPS
    q4 = dense_pos.shape[0]
    width = LANES // groups
    lane_grp = lax.broadcasted_iota(jnp.int32, (q4, LANES), 1) // width

    def by_group(vals):
        out = vals[groups - 1]
        for g in range(groups - 2, -1, -1):
            out = jnp.where(lane_grp == g, vals[g], out)
        return out

    ang = dense_pos * invf
    tables = []
    for dense in (jnp.cos(ang), jnp.sin(ang)):
        rolled = [dense] + [pltpu.roll(dense, k * width, axis=1) for k in range(1, groups)]
        tables.append(jnp.concatenate(
            [by_group([rolled[(j - g) % groups] for j in range(groups)]) for g in range(groups)],
            axis=0))
    return tables[0], tables[1] * sign


def _lru_rates(lam):
    nlam = -lam
    softplus = jnp.maximum(nlam, 0.0) + jnp.log1p(jnp.exp(-jnp.abs(nlam)))
    return (-0.5 * LRU_C * math.log2(math.e)) * softplus


def _lru_gate_block(xh, n, rate, wg_ref, br_ref, bi_ref):
    blk = xh.shape[1]
    c0, c1 = n * blk, (n + 1) * blk
    gates = _dot(xh.astype(BF16), wg_ref[n])
    tr = jnp.tanh(gates[:, 0:blk] + br_ref[:, c0:c1])
    ti = jnp.tanh(gates[:, blk:2 * blk] + bi_ref[:, c0:c1])
    a = jnp.exp2(tr * rate[:, c0:c1] + rate[:, c0:c1])
    gap = 1.0 - a * a
    root = jnp.where(gap > 0.0, gap * lax.rsqrt(gap), 0.0)
    return a, root * (ti * xh + xh)


def _lru_scan(a_s, u_s, h_s, out_ref, steps, reverse):
    def step(j, h):
        t = (steps - 1 - j) if reverse else j
        r0 = pl.multiple_of(t * SUBLANES, SUBLANES)
        h = a_s[:, pl.ds(r0, SUBLANES), :] * h + u_s[:, pl.ds(r0, SUBLANES), :]
        out_ref[:, pl.ds(r0, SUBLANES), :] = h
        return h

    h_s[...] = lax.fori_loop(0, steps, step, h_s[...], unroll=8)


def _inproj_kernel(h_ref, hp_ref, hn_ref, pos_ref, g_ref, w0_ref, w1_ref, w2_ref, w3_ref, qg_ref,
                   wqm_ref, kvg_ref, wkv_ref, invf_ref, sign_ref, cw_ref, cb_ref, wgl_ref,
                   br_ref, bi_ref, lam_ref,
                   zg_ref, ga_ref, gb_ref, q_ref, k_ref, v_ref, hf_ref, xa_ref,
                   xpad, a_s, u_s, h_s, hf_s, *, cols, q_scale):
    i = pl.program_id(0)
    nb, steps, d = h_ref.shape
    rows = nb * steps
    halo = hp_ref.shape[1]
    un = _rmsnorm(h_ref[...].reshape(rows, d), g_ref[...]).astype(BF16)

    def proj(name, lhs=un):
        part, c0, c1 = cols[name]
        return _dot(lhs, (w0_ref, w1_ref, w2_ref, w3_ref)[part][:, c0:c1])

    def per_batch(val):
        return val.reshape(nb, steps, val.shape[-1])

    @pl.when(i == 0)
    def _():
        h_s[...] = jnp.zeros_like(h_s)

    zx = proj("zx")
    h_halo = jnp.concatenate([hp_ref[...].reshape(nb * halo, d), hn_ref[...].reshape(nb * halo, d)])
    zx_halo = proj("zx", lhs=_rmsnorm(h_halo, g_ref[...]).astype(BF16))
    zx_prev = jnp.where(i > 0, zx_halo[0:nb * halo], 0.0)
    zx_next = jnp.where(i < pl.num_programs(0) - 1, zx_halo[nb * halo:], 0.0)
    first = halo * nb
    for n in range(LRU_BLOCKS):
        lanes = slice(n * LANES, (n + 1) * LANES)
        for b in range(nb):
            xpad[n, pl.ds(b, halo, stride=nb), :] = zx_prev[b * halo:(b + 1) * halo, lanes]
            xpad[n, pl.ds(first + b, steps, stride=nb), :] = zx[b * steps:(b + 1) * steps, lanes]
            xpad[n, pl.ds(first + rows + b, halo, stride=nb), :] = (
                zx_next[b * halo:(b + 1) * halo, lanes])

    rate = _lru_rates(lam_ref[...])

    def lru_block(n):
        lanes = slice(n * LANES, (n + 1) * LANES)
        xa = cb_ref[:, lanes]
        for k in range(CONV_WIDTH):
            r0 = first + (k - CONV_WIDTH // 2) * SUBLANES
            xa = xa + xpad[n, r0:r0 + rows, :] * cw_ref[k:k + 1, lanes]
        xa_ref[n] = xa.astype(xa_ref.dtype)
        a_s[n], u_s[n] = _lru_gate_block(xa, n, rate, wgl_ref, br_ref, bi_ref)

    def half_proj(name, out_ref, half):
        part, c0, c1 = cols[name]
        mid = (c0 + c1) // 2
        lo, hi = (c0, mid) if half == 0 else (mid, c1)
        w = (w0_ref, w1_ref, w2_ref, w3_ref)[part]
        out_ref[:, :, lo - c0:hi - c0] = per_batch(_dot(un, w[:, lo:hi]).astype(BF16))

    gate_outputs = [(name, ref, half) for name, ref in (("zg", zg_ref), ("ga", ga_ref),
                                                        ("gb", gb_ref)) for half in (0, 1)]
    for n, gate_output in enumerate(gate_outputs):
        lru_block(n)
        half_proj(*gate_output)
    first_late = len(gate_outputs)
    lru_block(first_late)

    cosv, sinv = _rope_tables(pos_ref[0], invf_ref[...], sign_ref[...])
    lane = lax.broadcasted_iota(jnp.int32, (rows, LANES), 1)
    low = lane < QK_ROPE
    table = jnp.where(low, cosv, sinv)

    def rotate(x, tab):
        y = x * tab
        return jnp.where(low, y + pltpu.roll(y, QK_ROPE, axis=1), 0.0)

    cqn = _rmsnorm(proj("cq"), qg_ref[...]).astype(BF16)
    qm = _dot(cqn, wqm_ref[...])
    q_table = table * q_scale
    for h in range(MLA_HEADS):
        m0 = h * QK_PAD
        q_ref[:, h, :, 0:QK_NOPE] = per_batch((qm[:, m0:m0 + QK_NOPE] * q_scale).astype(BF16))
        rope = rotate(qm[:, m0 + QK_NOPE:m0 + QK_PAD], q_table)
        q_ref[:, h, :, QK_NOPE:QK_PAD] = per_batch(rope.astype(BF16))

    for n in range(first_late + 1, LRU_BLOCKS):
        lru_block(n)

    ckvn =_rmsnorm(proj("ckv"), kvg_ref[...]).astype(BF16)
    kv = _dot(ckvn, wkv_ref[...])
    kpe = per_batch(jnp.where(lane == STAB_LANE - QK_NOPE, 1.0,
                              rotate(proj("kr"), table)).astype(BF16))
    for h in range(MLA_HEADS):
        c0 = h * (QK_NOPE + V_HEAD)
        k_ref[:, h, :, 0:QK_NOPE] = per_batch(kv[:, c0:c0 + QK_NOPE].astype(BF16))
        k_ref[:, h, :, QK_NOPE:QK_PAD] = kpe
        v_ref[:, h, :, :] = per_batch(kv[:, c0 + QK_NOPE:c0 + QK_NOPE + V_HEAD].astype(BF16))

    _lru_scan(a_s, u_s, h_s, hf_s, steps, reverse=False)
    hf_ref[...] = hf_s[...].astype(hf_ref.dtype)


def _inproj(h1, posf, g, w_parts, cols, qg, wqm, kvg, wkv, invf, sign, q_scale, lru_weights):
    b, s, d = h1.shape
    t = STEPS
    halo = SUBLANES
    width = cols["zx"][2] - cols["zx"][1]
    nblk = width // LANES
    rows = t * b
    bm = lambda x: pl.BlockSpec((b, t, x), lambda i: (0, i, 0))
    heads = lambda x: pl.BlockSpec((b, MLA_HEADS, t, x), lambda i: (0, 0, i, 0))
    tm = pl.BlockSpec((nblk, rows, LANES), lambda i: (0, i, 0))
    per_halo = t // halo
    prev = pl.BlockSpec((b, halo, d), lambda i: (0, jnp.maximum(i * per_halo - 1, 0), 0))
    nxt = pl.BlockSpec((b, halo, d), lambda i: (0, jnp.minimum((i + 1) * per_halo,
                                                               s // halo - 1), 0))
    weights = (g, *w_parts, qg, wqm, kvg, wkv, invf, sign, *lru_weights)
    kernel = functools.partial(_inproj_kernel, cols=cols, q_scale=q_scale)
    return pl.pallas_call(
        kernel,
        out_shape=[
            jax.ShapeDtypeStruct((b, s, width), BF16),
            jax.ShapeDtypeStruct((b, s, d), BF16),
            jax.ShapeDtypeStruct((b, s, d), BF16),
            jax.ShapeDtypeStruct((b, MLA_HEADS, s, QK_PAD), BF16),
            jax.ShapeDtypeStruct((b, MLA_HEADS, s, QK_PAD), BF16),
            jax.ShapeDtypeStruct((b, MLA_HEADS, s, V_HEAD), BF16),
            jax.ShapeDtypeStruct((nblk, s * b, LANES), BF16),
            jax.ShapeDtypeStruct((nblk, s * b, LANES), BF16),
        ],
        grid=(s // t,),
        in_specs=[bm(d), prev, nxt, pl.BlockSpec((1,) + posf.shape[1:], lambda i: (i, 0, 0))]
        + [_resident(a.shape) for a in weights],
        out_specs=[bm(width), bm(d), bm(d), heads(QK_PAD), heads(QK_PAD), heads(V_HEAD), tm, tm],
        scratch_shapes=[
            pltpu.VMEM((nblk, rows + 2 * halo * b, LANES), F32),
            pltpu.VMEM((nblk, rows, LANES), F32),
            pltpu.VMEM((nblk, rows, LANES), F32),
            pltpu.VMEM((nblk, SUBLANES, LANES), F32),
            pltpu.VMEM((nblk, rows, LANES), F32),
        ],
        compiler_params=_params(("arbitrary",)),
        name="inproj",
    )(h1, h1, h1, posf, *weights)


def _lru_bwd_kernel(xa_ref, hf_ref, wg_ref, br_ref, bi_ref, lam_ref, o_ref, a_s, u_s, h_s, hb_s,
                    *, steps):
    @pl.when(pl.program_id(0) == 0)
    def _():
        h_s[...] = jnp.zeros_like(h_s)

    rate = _lru_rates(lam_ref[...])
    for n in range(LRU_BLOCKS):
        a_s[n], u_s[n] = _lru_gate_block(xa_ref[n].astype(F32), n, rate, wg_ref, br_ref, bi_ref)
    _lru_scan(a_s, u_s, h_s, hb_s, steps, reverse=True)

    hb_s[...] = hb_s[...] + hf_ref[...].astype(F32)
    blk = xa_ref.shape[2]
    for n in range(LRU_BLOCKS):
        for b in range(SUBLANES):
            o_ref[b, :, n * blk:(n + 1) * blk] = (
                hb_s[n, pl.ds(b, steps, stride=SUBLANES), :].astype(o_ref.dtype))


def _lru_bwd(xa3, hf3, wg, br, bi, lam):
    nblk, n_rows, blk = xa3.shape
    steps = BWD_STEPS
    rows = steps * SUBLANES
    n_tiles = n_rows // rows
    cur = pl.BlockSpec((nblk, rows, blk), lambda i: (0, n_tiles - 1 - i, 0))
    weights = (wg, br, bi, lam)
    return pl.pallas_call(
        functools.partial(_lru_bwd_kernel, steps=steps),
        out_shape=jax.ShapeDtypeStruct((SUBLANES, n_rows // SUBLANES, nblk * blk), BF16),
        grid=(n_tiles,),
        in_specs=[cur, cur] + [_resident(a.shape) for a in weights],
        out_specs=pl.BlockSpec((SUBLANES, steps, nblk * blk), lambda i: (0, n_tiles - 1 - i, 0)),
        scratch_shapes=[
            pltpu.VMEM((nblk, rows, blk), F32),
            pltpu.VMEM((nblk, rows, blk), F32),
            pltpu.VMEM((nblk, SUBLANES, blk), F32),
            pltpu.VMEM((nblk, rows, blk), F32),
        ],
        compiler_params=_params(("arbitrary",)),
        name="lru_bwd",
    )(xa3, hf3, *weights)


def _attn_kernel(q_ref, k_ref, v_ref, o_ref, p_s, vt_s, kmax_s):
    dv = v_ref.shape[-1]
    tb = p_s.shape[-1]
    n_blocks = q_ref.shape[2] // tb
    lane = lax.broadcasted_iota(jnp.int32, (1, q_ref.shape[-1]), 1)

    @pl.when(pl.program_id(2) == 0)
    def _():
        vt_s[...] = v_ref[0, 0].astype(F32).T.astype(BF16)
        kf = k_ref[0, 0].astype(F32)
        ksq = jnp.max(jnp.sum(kf * kf, axis=1, keepdims=True), axis=0, keepdims=True)
        kmax_s[...] = jnp.broadcast_to(jnp.sqrt(ksq), kmax_s.shape)

    def block(i, exact):
        r0 = i * tb
        slot = i % p_s.shape[0]
        qf = q_ref[0, 0, r0:r0 + tb, :].astype(F32)
        bound = jnp.sqrt(jnp.sum(qf * qf, axis=1, keepdims=True)) * kmax_s[...]
        qs = jnp.where(lane == STAB_LANE, -bound, qf).astype(BF16)
        st = _dot_nt(k_ref[0, 0], qs)
        if exact:
            st = st - jnp.max(st, axis=0, keepdims=True)
        p = jnp.exp2(st)
        p_s[slot] = p.astype(BF16)
        denom = jnp.sum(p, axis=0, keepdims=True)
        ot = _dot(vt_s[...], p_s[slot])
        o_ref[0, r0:r0 + tb, :] = (ot / denom).T.astype(o_ref.dtype)
        return jnp.min(denom)

    smallest = block(0, exact=False)
    for i in range(1, n_blocks):
        smallest = jnp.minimum(smallest, block(i, exact=False))

    @pl.when(jnp.logical_not(smallest >= DENOM_FLOOR))
    def _():
        for i in range(n_blocks):
            block(i, exact=True)


def _attn(q, k, v):
    b, nh, s, dq = q.shape
    dv = v.shape[-1]
    tq = min(Q_STEP_BLOCKS * Q_BLOCK, s)
    head = lambda x: pl.BlockSpec((1, 1, s, x), lambda i, h, j: (i, h, 0, 0))
    return pl.pallas_call(
        _attn_kernel,
        out_shape=jax.ShapeDtypeStruct((b, s, nh * dv), BF16),
        grid=(b, nh, s // tq),
        in_specs=[pl.BlockSpec((1, 1, tq, dq), lambda i, h, j: (i, h, j, 0)), head(dq), head(dv)],
        out_specs=pl.BlockSpec((1, tq, dv), lambda i, h, j: (i, j, h)),
        scratch_shapes=[
            pltpu.VMEM((2, s, Q_BLOCK), BF16),
            pltpu.VMEM((dv, s), BF16),
            pltpu.VMEM((1, dq), F32),
        ],
        compiler_params=_params(("parallel", "parallel", "arbitrary")),
        name="attn",
    )(q, k, v)


def _tail_kernel(h_ref, ys_ref, zg_ref, ob_ref, ga_ref, gb_ref, p_ref, wl_ref, wm_ref, wo_ref,
                 g_ref, wg_ref, wu_ref, wd_ref, pg_ref, wpg_ref, wpp_ref, ppg_ref, fg_ref, o_ref):
    emb = _rmsnorm(_dot(p_ref[0].astype(BF16), wpp_ref[...]), ppg_ref[...])

    ya_in = ys_ref[0].astype(F32) * jax.nn.gelu(zg_ref[0].astype(F32))
    ya = _dot(ya_in.astype(BF16), wl_ref[...])
    yb = _dot(ob_ref[0], wm_ref[...])
    merged = (jax.nn.sigmoid(ga_ref[0].astype(F32)) * ya
              + jax.nn.sigmoid(gb_ref[0].astype(F32)) * yb)
    h = h_ref[0] + _dot(merged.astype(BF16), wo_ref[...])

    h = _swiglu_half(h, g_ref[...], wg_ref, wu_ref, wd_ref)
    gate = jax.nn.sigmoid(_dot(_rmsnorm(h, pg_ref[...]).astype(BF16), wpg_ref[...]))
    h = h + gate * emb
    o_ref[0] = _rmsnorm(h, fg_ref[...])


def _tail(h1, ysum, zg, ob, ga, gb, p, weights):
    b, s, d = h1.shape
    t = FFN_ROWS
    row = pl.BlockSpec((1, t, d), lambda i, j: (i, j, 0))
    prow = pl.BlockSpec((None, 1, t, p.shape[-1]), lambda i, j: (0, i, j, 0))
    return pl.pallas_call(
        _tail_kernel,
        out_shape=jax.ShapeDtypeStruct((b, s, d), F32),
        grid=(b, s // t),
        in_specs=[row] * 6 + [prow] + [_resident(a.shape) for a in weights],
        out_specs=row,
        compiler_params=_params(("parallel", "parallel")),
        name="tail",
    )(h1, ysum, zg, ob, ga, gb, p, *weights)


def _swap_halves(w):
    half = w.shape[-1] // 2
    return jnp.concatenate([w[..., half:], w[..., :half]], axis=-1)


def _in_weights(w_in, d, lru_w, q_lora, kv_lora):
    sizes = (lru_w, lru_w, q_lora, kv_lora, QK_ROPE, d, d)
    offs = [0]
    for sz in sizes:
        offs.append(offs[-1] + sz)
    kr = w_in[:, offs[4]:offs[5]]
    pieces = (w_in[:, :offs[4]], jnp.concatenate([kr, _swap_halves(kr)], axis=1),
              w_in[:, offs[5]:offs[6]], w_in[:, offs[6]:offs[7]])
    cols = {name: (0, offs[i], offs[i + 1]) for i, name in enumerate(("zx", "zg", "cq", "ckv"))}
    cols.update(kr=(1, 0, 2 * QK_ROPE), ga=(2, 0, d), gb=(3, 0, d))
    return tuple(piece.astype(BF16) for piece in pieces), cols


def _q_weights(w_uq):
    r = w_uq.shape[0]
    w = w_uq.reshape(r, MLA_HEADS, QK_NOPE + QK_ROPE)
    full = jnp.concatenate([w, _swap_halves(w[:, :, QK_NOPE:])], axis=-1)
    assert full.shape[-1] == QK_PAD
    return full.reshape(r, MLA_HEADS * QK_PAD).astype(BF16)


def kernel(x, p, positions, ffn1_norm, ffn1_w_gate, ffn1_w_up, ffn1_w_down, mix_norm, w_in, conv_w, conv_b, lru_w_r, lru_b_r, lru_w_i, lru_b_i, lru_lambda, w_lru_out, q_norm, w_uq, kv_norm, w_ukv, w_mla_out, w_o, ffn2_norm, ffn2_w_gate, ffn2_w_up, ffn2_w_down, ple_norm, ple_w_gate, ple_w_proj, ple_proj_norm, final_norm):
    b, s, d = x.shape
    depth = ffn1_norm.shape[0]
    lru_w = conv_w.shape[-1]
    q_lora, kv_lora = q_norm.shape[-1], kv_norm.shape[-1]
    assert b == SUBLANES and lru_w == d
    assert depth == 1
    assert s % FFN_ROWS == 0 and s % STEPS == 0 and s % BWD_STEPS == 0
    assert s % Q_BLOCK == 0 and s % min(Q_STEP_BLOCKS * Q_BLOCK, s) == 0

    row = lambda v: v.reshape(1, -1).astype(F32)
    posf = _dense_positions(positions, STEPS)
    half = QK_ROPE // 2
    inv_freq = ROPE_THETA ** (-jnp.arange(0, QK_ROPE, 2, dtype=F32) / QK_ROPE)
    invf = jnp.tile(inv_freq, LANES // half).reshape(1, LANES)
    sign = jnp.tile(jnp.concatenate([-jnp.ones(half, F32), jnp.ones(half, F32)]),
                    LANES // QK_ROPE).reshape(1, LANES)
    q_scale = (QK_NOPE + QK_ROPE) ** -0.5 * math.log2(math.e)

    i = 0
    h1 = _ffn(x, row(ffn1_norm[i]), ffn1_w_gate[i].astype(BF16), ffn1_w_up[i].astype(BF16),
              ffn1_w_down[i].astype(BF16))

    w_parts, cols = _in_weights(w_in[i], d, lru_w, q_lora, kv_lora)
    wqm = _q_weights(w_uq[i])
    lru_args = lambda dr: (
        jnp.concatenate([lru_w_r[i, dr], lru_w_i[i, dr]], axis=-1).astype(BF16),
        row(0.5 * lru_b_r[i, dr]), row(0.5 * lru_b_i[i, dr]), row(lru_lambda[i, dr]))
    conv = (0.5 * conv_w[i].astype(F32), row(0.5 * conv_b[i]))
    zg, ga, gb, q, k, v, hf3, xa3 = _inproj(
        h1, posf, row(mix_norm[i]), w_parts, cols, row(q_norm[i]), wqm, row(kv_norm[i]),
        w_ukv[i].astype(BF16), invf, sign, q_scale, (*conv, *lru_args(0)))
    ysum = _lru_bwd(xa3, hf3, *lru_args(1))

    ob = _attn(q, k, v)

    tail_weights = (
        w_lru_out[i].astype(BF16), w_mla_out[i].astype(BF16), w_o[i].astype(BF16),
        row(ffn2_norm[i]), ffn2_w_gate[i].astype(BF16), ffn2_w_up[i].astype(BF16),
        ffn2_w_down[i].astype(BF16), row(ple_norm[i]), ple_w_gate[i].astype(BF16),
        ple_w_proj[i].astype(BF16), row(ple_proj_norm[i]), row(final_norm))
    return _tail(h1, ysum, zg, ob, ga, gb, p, tail_weights)
```
